```python
import jax
import jax.numpy as jnp
from jax import lax
import numpy as np


D_MODEL = 1024
BATCH = 8
SEQ = 2048
DEPTH = 4

GRID_W = 64
HEAD_DIM = 64
ATT_HEADS = 8
ATT_KV_HEADS = 2
ATT_WIDTH = ATT_HEADS * HEAD_DIM
KV_WIDTH = ATT_KV_HEADS * HEAD_DIM
Q_BLOCK = 128
ROPE_THETA = 10000.0
LRU_WIDTH = D_MODEL // 4
LRU_BLOCKS = 4
LRU_BLOCK_DIM = LRU_WIDTH // LRU_BLOCKS
CONV_WIDTH = 4
LRU_C = 8.0
MLSTM_HEADS = 4
MLSTM_DIM = 64
MLSTM_WIDTH = MLSTM_HEADS * MLSTM_DIM
MLSTM_CHUNK = 128
D_MIX = ATT_WIDTH + LRU_WIDTH + MLSTM_WIDTH
IN_SPLITS = (ATT_WIDTH, KV_WIDTH, KV_WIDTH, LRU_WIDTH, LRU_WIDTH, MLSTM_WIDTH, MLSTM_WIDTH, MLSTM_WIDTH, MLSTM_WIDTH, 4 * MLSTM_HEADS)
D_IN = sum(IN_SPLITS)
N_EXPERTS = 16
CAPACITY_FACTOR = 2
D_EXPERT = 2 * D_MODEL
EPS = 1e-6

kernel_name = 'hybrid_head_group_encoder_ec_moe'

F32 = jnp.float32


def rms_norm(x, g):
    xf = x.astype(F32)
    y = xf * lax.rsqrt(jnp.mean(xf * xf, axis=-1, keepdims=True) + EPS)
    return (y * g.astype(F32)).astype(x.dtype)


def split_columns(p):
    idx = [int(i) for i in np.cumsum(np.array(IN_SPLITS))[:-1]]
    return jnp.split(p, idx, axis=-1)


def axial_rope_tables(seq_len):
    rows = seq_len // GRID_W
    t = jnp.arange(rows * GRID_W)
    row = (t // GRID_W).astype(F32)
    col = (t % GRID_W).astype(F32)
    n_freq = HEAD_DIM // 4
    inv = 1.0 / (ROPE_THETA ** (jnp.arange(n_freq, dtype=F32) / n_freq))
    ang = jnp.concatenate([row[:, None] * inv, col[:, None] * inv], axis=-1)
    return jnp.cos(ang), jnp.sin(ang)


def apply_rope(x, cos, sin):
    half = HEAD_DIM // 2
    x1, x2 = x[..., :half], x[..., half:]
    return jnp.concatenate([x1 * cos - x2 * sin, x2 * cos + x1 * sin], axis=-1)


def axial_gqa_attention(q, k, v, q_g, k_g):
    B, S, _ = q.shape
    dt = q.dtype
    q = rms_norm(q.reshape(B, S, ATT_HEADS, HEAD_DIM), q_g)
    k = rms_norm(k.reshape(B, S, ATT_KV_HEADS, HEAD_DIM), k_g)
    v = v.reshape(B, S, ATT_KV_HEADS, HEAD_DIM)
    cos, sin = axial_rope_tables(S)
    cos, sin = cos[:, None, :], sin[:, None, :]
    q = apply_rope(q.astype(F32), cos, sin).astype(dt)
    k = apply_rope(k.astype(F32), cos, sin).astype(dt)
    grp = ATT_HEADS // ATT_KV_HEADS
    qb = q.reshape(B, S // Q_BLOCK, Q_BLOCK, ATT_KV_HEADS, grp, HEAD_DIM)
    qb = jnp.moveaxis(qb, 1, 0)
    scale = HEAD_DIM ** -0.5

    def one_block(qblk):
        s = jnp.einsum('bqhgd,bkhd->bhgqk', qblk, k).astype(F32) * scale
        p = jax.nn.softmax(s, axis=-1).astype(v.dtype)
        return jnp.einsum('bhgqk,bkhd->bqhgd', p, v)

    o = lax.map(one_block, qb)
    return jnp.moveaxis(o, 0, 1).reshape(B, S, ATT_WIDTH)


def centred_depthwise_conv(x, w, b):
    S = x.shape[1]
    left = CONV_WIDTH // 2
    xp = jnp.pad(x, ((0, 0), (left, CONV_WIDTH - 1 - left), (0, 0)))
    y = b
    for j in range(CONV_WIDTH):
        y = y + xp[:, j:j + S, :] * w[j]
    return y


def block_diag_linear(x, w, b):
    B, S, C = x.shape
    xb = x.reshape(B, S, LRU_BLOCKS, LRU_BLOCK_DIM)
    return jnp.einsum('bsnd,nde->bsne', xb, w).reshape(B, S, C) + b


def linear_scan(a, u):
    def comb(l, r):
        al, ul = l
        ar, ur = r
        return al * ar, ar * ul + ur
    _, h = lax.associative_scan(comb, (a, u), axis=1)
    return h


def rglru_direction(xc, wa, ba, wx, bx, lam, reverse):
    if reverse:
        xc = jnp.flip(xc, axis=1)
    r = jax.nn.sigmoid(block_diag_linear(xc, wa, ba).astype(F32))
    i = jax.nn.sigmoid(block_diag_linear(xc, wx, bx).astype(F32))
    log_a = -LRU_C * r * jax.nn.softplus(-lam.astype(F32))
    u = jnp.sqrt(-jnp.expm1(2.0 * log_a)) * (i * xc.astype(F32))
    h = linear_scan(jnp.exp(log_a), u)
    if reverse:
        h = jnp.flip(h, axis=1)
    return h


def rglru_mixer(xb, gate, conv_w, conv_b, wa, ba, wx, bx, lam):
    xc = centred_depthwise_conv(xb, conv_w, conv_b)
    h = rglru_direction(xc, wa[0], ba[0], wx[0], bx[0], lam[0], False) \
        + rglru_direction(xc, wa[1], ba[1], wx[1], bx[1], lam[1], True)
    return h.astype(xb.dtype) * jax.nn.gelu(gate)


def mlstm_chunkwise(q, k, v, i_pre, f_pre):
    B, H, S, D = q.shape
    L = MLSTM_CHUNK
    NC = S // L
    qc = q.reshape(B, H, NC, L, D)
    kc = k.reshape(B, H, NC, L, D) * (D ** -0.5)
    vc = v.reshape(B, H, NC, L, D)
    ig = i_pre.reshape(B, H, NC, L)
    bcum = jnp.cumsum(jax.nn.log_sigmoid(f_pre).reshape(B, H, NC, L), axis=-1)
    mask = jnp.tril(jnp.ones((L, L), dtype=bool))
    log_d = jnp.where(mask, bcum[..., :, None] - bcum[..., None, :] + ig[..., None, :], -jnp.inf)
    g = bcum[..., -1]
    log_w_end = g[..., None] - bcum + ig
    m_loc = jnp.max(log_w_end, axis=-1)
    w_end = jnp.exp(log_w_end - m_loc[..., None])
    c_loc = jnp.einsum('bhcl,bhcld,bhcle->bhcde', w_end, vc, kc)
    n_loc = jnp.einsum('bhcl,bhcle->bhce', w_end, kc)

    def step(carry, inp):
        c_st, n_st, m_st = carry
        c_l, n_l, m_l, g_c = inp
        m_new = jnp.maximum(g_c + m_st, m_l)
        a = jnp.exp(g_c + m_st - m_new)
        bb = jnp.exp(m_l - m_new)
        c_new = a[..., None, None] * c_st + bb[..., None, None] * c_l
        n_new = a[..., None] * n_st + bb[..., None] * n_l
        return (c_new, n_new, m_new), (c_st, n_st, m_st)

    init = (jnp.zeros((B, H, D, D), F32), jnp.zeros((B, H, D), F32), jnp.zeros((B, H), F32))
    xs = (jnp.moveaxis(c_loc, 2, 0), jnp.moveaxis(n_loc, 2, 0), jnp.moveaxis(m_loc, 2, 0), jnp.moveaxis(g, 2, 0))
    _, (c_prev, n_prev, m_prev) = lax.scan(step, init, xs)
    c_prev = jnp.moveaxis(c_prev, 0, 2)
    n_prev = jnp.moveaxis(n_prev, 0, 2)
    m_prev = jnp.moveaxis(m_prev, 0, 2)

    log_inter = bcum + m_prev[..., None]
    m_out = jnp.maximum(jnp.max(log_d, axis=-1), log_inter)
    s = jnp.einsum('bhcid,bhcjd->bhcij', qc, kc) * jnp.exp(log_d - m_out[..., None])
    inter = jnp.exp(log_inter - m_out)
    num = jnp.einsum('bhcij,bhcjd->bhcid', s, vc) + inter[..., None] * jnp.einsum('bhcde,bhcie->bhcid', c_prev, qc)
    den = jnp.sum(s, axis=-1) + inter * jnp.einsum('bhce,bhcie->bhci', n_prev, qc)
    h = num / jnp.maximum(jnp.abs(den), jnp.exp(-m_out))[..., None]
    return h.reshape(B, H, S, D)


def mlstm_mixer(q, k, v, o_pre, gates, f_bias, head_g):
    B, S, _ = q.shape
    dt = q.dtype

    def heads(t):
        return t.reshape(B, S, MLSTM_HEADS, MLSTM_DIM).transpose(0, 2, 1, 3).astype(F32)

    qh, kh, vh = heads(q), heads(k), heads(v)
    gt = gates.astype(F32).reshape(B, S, 4, MLSTM_HEADS).transpose(2, 0, 3, 1)
    fb = f_bias.astype(F32)
    h_fw = mlstm_chunkwise(qh, kh, vh, gt[0], gt[1] + fb[0][None, :, None])
    flip = lambda t: jnp.flip(t, axis=2)
    h_bw = jnp.flip(mlstm_chunkwise(flip(qh), flip(kh), flip(vh), jnp.flip(gt[2], axis=-1),
                                    jnp.flip(gt[3] + fb[1][None, :, None], axis=-1)), axis=2)
    h = rms_norm((h_fw + h_bw).transpose(0, 2, 1, 3), head_g)
    return h.reshape(B, S, MLSTM_WIDTH).astype(dt) * jax.nn.sigmoid(o_pre)


def expert_choice_moe(h, w_router, w_gate, w_up, w_down):
    B, S, D = h.shape
    cap = CAPACITY_FACTOR * S // N_EXPERTS
    aff = jax.nn.softmax(jnp.einsum('bsd,de->bse', h, w_router).astype(F32), axis=-1)
    gate, idx = lax.top_k(jnp.swapaxes(aff, 1, 2), cap)
    xg = jax.vmap(lambda hb, ib: hb[ib])(h, idx)
    a = jnp.einsum('becd,edf->becf', xg, w_gate)
    u = jnp.einsum('becd,edf->becf', xg, w_up)
    y = jnp.einsum('becf,efd->becd', jax.nn.silu(a) * u, w_down) * gate[..., None].astype(h.dtype)
    return jax.vmap(lambda yb, ib: jnp.zeros((S, D), yb.dtype).at[ib.reshape(-1)].add(yb.reshape(-1, D)))(y, idx)


def setup_inputs(seed: int = 0) -> dict:
    key = jax.random.key(seed)
    ks = jax.random.split(key, 24)
    nrm = lambda k, shape, s: jax.random.normal(k, shape, F32) * s
    gain = lambda k, shape: 1.0 + 0.02 * jax.random.normal(k, shape, F32)
    u = jax.random.uniform(ks[11], (DEPTH, 2, LRU_WIDTH), F32, 0.9, 0.999)
    a0 = u ** (1.0 / LRU_C)
    lam = jnp.log(a0) - jnp.log1p(-a0)
    f_bias = jnp.broadcast_to(jnp.linspace(3.0, 6.0, MLSTM_HEADS), (DEPTH, 2, MLSTM_HEADS)) \
        + 0.1 * jax.random.normal(ks[12], (DEPTH, 2, MLSTM_HEADS), F32)
    return {
        'x': jax.random.normal(ks[0], (BATCH, SEQ, D_MODEL), F32),
        'norm1_g': gain(ks[1], (DEPTH, D_MODEL)),
        'w_in': nrm(ks[2], (DEPTH, D_MODEL, D_IN), D_MODEL ** -0.5),
        'q_norm_g': gain(ks[3], (DEPTH, HEAD_DIM)),
        'k_norm_g': gain(ks[4], (DEPTH, HEAD_DIM)),
        'conv_w': nrm(ks[5], (DEPTH, CONV_WIDTH, LRU_WIDTH), CONV_WIDTH ** -0.5),
        'conv_b': nrm(ks[6], (DEPTH, LRU_WIDTH), 0.02),
        'lru_wa': nrm(ks[7], (DEPTH, 2, LRU_BLOCKS, LRU_BLOCK_DIM, LRU_BLOCK_DIM), LRU_BLOCK_DIM ** -0.5),
        'lru_ba': nrm(ks[8], (DEPTH, 2, LRU_WIDTH), 0.02),
        'lru_wx': nrm(ks[9], (DEPTH, 2, LRU_BLOCKS, LRU_BLOCK_DIM, LRU_BLOCK_DIM), LRU_BLOCK_DIM ** -0.5),
        'lru_bx': nrm(ks[10], (DEPTH, 2, LRU_WIDTH), 0.02),
        'lru_lambda': lam,
        'mlstm_f_bias': f_bias,
        'mlstm_norm_g': gain(ks[13], (DEPTH, MLSTM_HEADS, MLSTM_DIM)),
        'att_out_g': gain(ks[14], (DEPTH, ATT_WIDTH)),
        'lru_out_g': gain(ks[15], (DEPTH, LRU_WIDTH)),
        'w_out': nrm(ks[16], (DEPTH, D_MIX, D_MODEL), D_MIX ** -0.5),
        'norm2_g': gain(ks[17], (DEPTH, D_MODEL)),
        'w_router': nrm(ks[18], (DEPTH, D_MODEL, N_EXPERTS), D_MODEL ** -0.5),
        'w_expert_gate': nrm(ks[19], (DEPTH, N_EXPERTS, D_MODEL, D_EXPERT), D_MODEL ** -0.5),
        'w_expert_up': nrm(ks[20], (DEPTH, N_EXPERTS, D_MODEL, D_EXPERT), D_MODEL ** -0.5),
        'w_expert_down': nrm(ks[21], (DEPTH, N_EXPERTS, D_EXPERT, D_MODEL), D_EXPERT ** -0.5),
    }


def reference(x, norm1_g, w_in, q_norm_g, k_norm_g, conv_w, conv_b, lru_wa, lru_ba, lru_wx, lru_bx,
              lru_lambda, mlstm_f_bias, mlstm_norm_g, att_out_g, lru_out_g, w_out, norm2_g, w_router,
              w_expert_gate, w_expert_up, w_expert_down):
    for l in range(DEPTH):
        h = rms_norm(x, norm1_g[l])
        aq, ak, av, lx, lg, mq, mk, mv, mo, mg = split_columns(h @ w_in[l])
        y_att = axial_gqa_attention(aq, ak, av, q_norm_g[l], k_norm_g[l])
        y_lru = rglru_mixer(lx, lg, conv_w[l], conv_b[l], lru_wa[l], lru_ba[l], lru_wx[l], lru_bx[l], lru_lambda[l])
        y_mls = mlstm_mixer(mq, mk, mv, mo, mg, mlstm_f_bias[l], mlstm_norm_g[l])
        y = jnp.concatenate([rms_norm(y_att, att_out_g[l]), rms_norm(y_lru, lru_out_g[l]), y_mls], axis=-1)
        x = x + y @ w_out[l]
        h2 = rms_norm(x, norm2_g[l])
        x = x + expert_choice_moe(h2, w_router[l], w_expert_gate[l], w_expert_up[l], w_expert_down[l])
    return x
```

```python
import functools

import numpy as np
import jax
import jax.numpy as jnp
from jax import lax
from jax.experimental import pallas as pl
from jax.experimental.pallas import tpu as pltpu

F32 = jnp.float32
BF16 = jnp.bfloat16
I32 = jnp.int32

D_MODEL = 1024
GRID_W = 64
HEAD_DIM = 64
ATT_HEADS = 8
ATT_KV_HEADS = 2
ATT_GROUP = ATT_HEADS // ATT_KV_HEADS
ATT_WIDTH = ATT_HEADS * HEAD_DIM
KV_WIDTH = ATT_KV_HEADS * HEAD_DIM
ROPE_THETA = 10000.0
LRU_WIDTH = 256
LRU_BLOCKS = 4
LRU_BLOCK_DIM = LRU_WIDTH // LRU_BLOCKS
CONV_WIDTH = 4
LRU_C = 8.0
MLSTM_HEADS = 4
MLSTM_DIM = 64
MLSTM_WIDTH = MLSTM_HEADS * MLSTM_DIM
MLSTM_CHUNK = 128
MLSTM_DX = MLSTM_DIM + 8
N_GATES = 4 * MLSTM_HEADS
N_EXPERTS = 16
CAPACITY_FACTOR = 2
D_EXPERT = 2 * D_MODEL
EPS = 1e-6

LANES = 128
D_IN = ATT_WIDTH + 2 * KV_WIDTH + 2 * LRU_WIDTH + 4 * MLSTM_WIDTH + N_GATES
D_IN_PAD = ATT_WIDTH + 2 * KV_WIDTH + 2 * LRU_WIDTH + 4 * MLSTM_WIDTH + LANES
VMEM_LIMIT = 56 * 1024 * 1024

ROW_TILE = 512
ATT_K_TILE = 1024
ATT_Q_TILE = 256
FFN_F_TILE = 512
FFN_M_TILE = 512
COMBINE_ROWS = 512

NT_DIMS = (((1,), (1,)), ((), ()))


def _params(*sem):
    return pltpu.CompilerParams(dimension_semantics=sem, vmem_limit_bytes=VMEM_LIMIT)


def _split3(x):
    hi = x.astype(BF16)
    r1 = x - hi.astype(F32)
    mid = r1.astype(BF16)
    lo = (r1 - mid.astype(F32)).astype(BF16)
    return hi, mid, lo


def _dot_exact_rhs(x, m):
    hi, mid, lo = _split3(x)
    d = lambda a: jnp.dot(a, m, preferred_element_type=F32)
    return d(hi) + d(mid) + d(lo)


def _dot_nt_f32(a, b):
    ah, am, al = _split3(a)
    bh, bm, bl = _split3(b)
    d = lambda p, q: lax.dot_general(p, q, NT_DIMS, preferred_element_type=F32)
    return d(ah, bh) + (d(ah, bm) + d(am, bh)) + (d(am, bm) + d(ah, bl) + d(al, bh))


def _rms(x, g):
    return x * lax.rsqrt(jnp.mean(x * x, axis=-1, keepdims=True) + EPS) * g


def _softplus(x):
    return jnp.maximum(x, 0.0) + jnp.log1p(jnp.exp(-jnp.abs(x)))


def _inproj_kernel(x_ref, g_ref, w_ref, cos_ref, sin_ref, qg_ref, kg_ref, gm_ref,
                   q_ref, k_ref, v_ref, lx_ref, lg_ref, mq_ref, mk_ref, mv_ref, mo_ref, mg_ref):
    h = _rms(x_ref[...], g_ref[...]).astype(BF16)
    p = jnp.dot(h, w_ref[...], preferred_element_type=F32)
    cos = cos_ref[...]
    sin = sin_ref[...]
    gm = gm_ref[...]
    lane = lax.broadcasted_iota(I32, cos.shape, 1)
    first_half = (lane % HEAD_DIM) < (HEAD_DIM // 2)

    def norm_rope(t, gain):
        y = t * lax.rsqrt(_dot_exact_rhs(t * t, gm) + EPS) * gain
        swapped = jnp.where(first_half, pltpu.roll(y, LANES - HEAD_DIM // 2, 1),
                            pltpu.roll(y, HEAD_DIM // 2, 1))
        return y * cos + swapped * sin

    scale = HEAD_DIM ** -0.5
    for j in range(ATT_WIDTH // LANES):
        sl = slice(j * LANES, (j + 1) * LANES)
        q_ref[:, sl] = (norm_rope(p[:, sl], qg_ref[...]) * scale).astype(BF16)
    o = ATT_WIDTH
    k_ref[...] = norm_rope(p[:, o:o + KV_WIDTH], kg_ref[...]).astype(BF16)
    o += KV_WIDTH
    v_ref[...] = p[:, o:o + KV_WIDTH].astype(BF16)
    o += KV_WIDTH
    for ref in (lx_ref, lg_ref, mq_ref, mk_ref, mv_ref, mo_ref):
        ref[...] = p[:, o:o + LRU_WIDTH]
        o += LRU_WIDTH
    mg_ref[...] = p[:, o:o + LANES]


def _inproj(x2d, g, w_pad, cos_t, sin_t, qg, kg, gm, seq):
    t = x2d.shape[0]
    tm = ROW_TILE
    n_seq = seq // tm
    row = lambda i: (i, 0)
    fixed = lambda i: (0, 0)
    tab = lambda i: (i % n_seq, 0)
    widths = [(ATT_WIDTH, BF16), (KV_WIDTH, BF16), (KV_WIDTH, BF16)] + \
             [(LRU_WIDTH, F32)] * 6 + [(LANES, F32)]
    return pl.pallas_call(
        _inproj_kernel,
        grid=(t // tm,),
        in_specs=[
            pl.BlockSpec((tm, D_MODEL), row),
            pl.BlockSpec((1, D_MODEL), fixed),
            pl.BlockSpec((D_MODEL, D_IN_PAD), fixed),
            pl.BlockSpec((tm, LANES), tab),
            pl.BlockSpec((tm, LANES), tab),
            pl.BlockSpec((1, LANES), fixed),
            pl.BlockSpec((1, LANES), fixed),
            pl.BlockSpec((LANES, LANES), fixed),
        ],
        out_specs=[pl.BlockSpec((tm, w), row) for w, _ in widths],
        out_shape=[jax.ShapeDtypeStruct((t, w), dt) for w, dt in widths],
        compiler_params=_params("parallel"),
        name="inproj",
    )(x2d, g, w_pad, cos_t, sin_t, qg, kg, gm)


def _attn_kernel(q_ref, k_ref, v_ref, o_ref):
    tq = q_ref.shape[0]
    seq = k_ref.shape[0]
    kb = ATT_K_TILE
    for h in range(ATT_KV_HEADS):
        hl = slice(h * HEAD_DIM, (h + 1) * HEAD_DIM)
        heads = [h * ATT_GROUP + g for g in range(ATT_GROUP)]
        qs = jnp.concatenate([q_ref[:, a * HEAD_DIM:(a + 1) * HEAD_DIM] for a in heads], axis=0)
        m = l = acc = None
        for j in range(seq // kb):
            ks = slice(j * kb, (j + 1) * kb)
            s = lax.dot_general(qs, k_ref[ks, hl], NT_DIMS, preferred_element_type=F32)
            m_blk = jnp.max(s, axis=-1, keepdims=True)
            if j == 0:
                m = m_blk
                p = jnp.exp(s - m)
                l = jnp.sum(p, axis=-1, keepdims=True)
                acc = jnp.dot(p.astype(BF16), v_ref[ks, hl], preferred_element_type=F32)
            else:
                m_new = jnp.maximum(m, m_blk)
                alpha = jnp.exp(m - m_new)
                p = jnp.exp(s - m_new)
                l = alpha * l + jnp.sum(p, axis=-1, keepdims=True)
                acc = alpha * acc + jnp.dot(p.astype(BF16), v_ref[ks, hl], preferred_element_type=F32)
                m = m_new
        o = acc / l
        for g, a in enumerate(heads):
            o_ref[:, a * HEAD_DIM:(a + 1) * HEAD_DIM] = o[g * tq:(g + 1) * tq]


def _attention(q, k, v, batch, seq):
    tq = ATT_Q_TILE
    nq = seq // tq
    return pl.pallas_call(
        _attn_kernel,
        grid=(batch, nq),
        in_specs=[
            pl.BlockSpec((tq, ATT_WIDTH), lambda b, i: (b * nq + i, 0)),
            pl.BlockSpec((seq, KV_WIDTH), lambda b, i: (b, 0)),
            pl.BlockSpec((seq, KV_WIDTH), lambda b, i: (b, 0)),
        ],
        out_specs=pl.BlockSpec((tq, ATT_WIDTH), lambda b, i: (b * nq + i, 0)),
        out_shape=jax.ShapeDtypeStruct((batch * seq, ATT_WIDTH), F32),
        compiler_params=_params("parallel", "parallel"),
        name="attention",
    )(q, k, v)


def _lru_kernel(x_ref, gate_ref, cw_ref, cb_ref, w_ref, b_ref, lam_ref, o_ref):
    x = x_ref[...]
    seq = x.shape[0]
    row = lax.broadcasted_iota(I32, x.shape, 0)

    def earlier(a, d, fill):
        return jnp.where(row >= d, pltpu.roll(a, d, 0), fill)

    def later(a, d, fill):
        return jnp.where(row < seq - d, pltpu.roll(a, seq - d, 0), fill)

    xc = cb_ref[...] + earlier(x, 2, 0.0) * cw_ref[0:1, :]
    xc = xc + earlier(x, 1, 0.0) * cw_ref[1:2, :]
    xc = xc + x * cw_ref[2:3, :]
    xc = xc + later(x, 1, 0.0) * cw_ref[3:4, :]

    z = jnp.dot(xc.astype(BF16), w_ref[...], preferred_element_type=F32) + b_ref[...]
    h = None
    for dr, shift in enumerate((earlier, later)):
        o = dr * 2 * LRU_WIDTH
        r = jax.nn.sigmoid(z[:, o:o + LRU_WIDTH])
        i = jax.nn.sigmoid(z[:, o + LRU_WIDTH:o + 2 * LRU_WIDTH])
        log_a = -LRU_C * r * _softplus(-lam_ref[dr:dr + 1, :])
        a = jnp.exp(log_a)
        u = jnp.sqrt(1.0 - jnp.exp(2.0 * log_a)) * (i * xc)
        d = 1
        while d < seq:
            u = a * shift(u, d, 0.0) + u
            if 2 * d < seq:
                a = a * shift(a, d, 1.0)
            d *= 2
        h = u if h is None else h + u
    o_ref[...] = h * jax.nn.gelu(gate_ref[...])


def _lru(lx, lg, cw, cb, w_bd, b_all, lam, batch, seq):
    blk = lambda b: (b, 0)
    fixed = lambda b: (0, 0)
    return pl.pallas_call(
        _lru_kernel,
        grid=(batch,),
        in_specs=[
            pl.BlockSpec((seq, LRU_WIDTH), blk),
            pl.BlockSpec((seq, LRU_WIDTH), blk),
            pl.BlockSpec((CONV_WIDTH, LRU_WIDTH), fixed),
            pl.BlockSpec((1, LRU_WIDTH), fixed),
            pl.BlockSpec((LRU_WIDTH, 4 * LRU_WIDTH), fixed),
            pl.BlockSpec((1, 4 * LRU_WIDTH), fixed),
            pl.BlockSpec((2, LRU_WIDTH), fixed),
        ],
        out_specs=pl.BlockSpec((seq, LRU_WIDTH), blk),
        out_shape=jax.ShapeDtypeStruct((batch * seq, LRU_WIDTH), F32),
        compiler_params=_params("parallel"),
        name="rglru",
    )(lx, lg, cw, cb, w_bd, b_all, lam)


def _mlstm_kernel(q_ref, k_ref, v_ref, o_ref, g_ref, fbt_ref, hg_ref, gm_ref, y_ref,
                  qbd_ref, kbd_ref, vx_ref, vxb_ref, bct_ref, row_ref, tot0_ref, cloc_ref, cst_ref, hf_ref, hb_ref):
    seq = q_ref.shape[0]
    L = MLSTM_CHUNK
    D = MLSTM_DIM
    DX = MLSTM_DX
    H = MLSTM_HEADS
    NG = N_GATES
    nc = seq // L
    ri = lax.broadcasted_iota(I32, (L, L), 0)
    ci = lax.broadcasted_iota(I32, (L, L), 1)
    tri_pre = jnp.where(ri <= ci, 1.0, 0.0).astype(BF16)
    tri_suf = jnp.where(ri >= ci, 1.0, 0.0).astype(BF16)
    gr = ri[:NG]
    is_f = (gr // H) % 2 == 1
    is_fwd = gr < 2 * H
    neg_inf = jnp.float32(-jnp.inf)
    ones_row = jnp.where(lax.broadcasted_iota(I32, (DX - D, L), 0) == 0, 1.0, 0.0)
    pad_rows = jnp.zeros((L - 3 * NG, L), F32)
    zero_qblk = jnp.zeros((D, L), BF16)
    zero_st = jnp.zeros((L, L), BF16)
    lane_w = lax.broadcasted_iota(I32, (1, H * D), 1)
    sel_r = lax.broadcasted_iota(I32, (L, 2 * H * L), 0)
    sel_n = lax.broadcasted_iota(I32, (L, 2 * H * L), 1) // L
    sel_col = (sel_n // H) * 2 * H + H + sel_n % H
    sel_all = jnp.where(sel_r % NG == sel_col, jnp.where(sel_r < 3 * NG, 1.0, 0.0), 0.0).astype(BF16)
    head_of_lane = lax.broadcasted_iota(I32, (L, H * D), 1) // D

    def per_head_rows(vals):
        out = jnp.zeros((1, H * D), F32)
        for h, v in enumerate(vals):
            v2 = jnp.concatenate([v] * (H * D // L), axis=-1)
            out = jnp.where(lane_w // D == h, v2, out)
        return out

    def prepare(c, _):
        rows = pl.ds(pl.multiple_of(c * L, L), L)
        gt = g_ref[rows, :].T[:NG]
        lf = jnp.where(is_f, -_softplus(-(gt + fbt_ref[...])), 0.0)
        bct = jnp.where(is_fwd, _dot_exact_rhs(lf, tri_pre), _dot_exact_rhs(lf, tri_suf))
        rt = pltpu.roll(gt, H, 0) - bct
        bct_ref[c] = bct
        hi, mid, lo = _split3(rt)
        stacked = jnp.concatenate([hi.astype(F32), mid.astype(F32), lo.astype(F32), pad_rows], axis=0)
        r_all = jnp.dot(stacked.T.astype(BF16), sel_all, preferred_element_type=F32)

        qt = q_ref[rows, :].T.astype(BF16)
        for h in range(H):
            blocks = [zero_qblk] * H
            blocks[h] = qt[h * D:(h + 1) * D]
            qbd_ref[c, h * D:(h + 1) * D, :] = jnp.concatenate(blocks, axis=-1)
        ks = (k_ref[rows, :] * (D ** -0.5)).astype(BF16)
        for h in range(H):
            kbd_ref[c, h * L:(h + 1) * L, :] = jnp.where(head_of_lane == h, ks, jnp.zeros_like(ks))
        vt = v_ref[rows, :].T
        vx = jnp.concatenate([jnp.concatenate([vt[h * D:(h + 1) * D], ones_row], axis=0) for h in range(H)], axis=-1)
        vxb = vx.astype(BF16)
        vx_ref[c] = vx
        vxb_ref[c] = vxb
        kq = jnp.dot(ks, qbd_ref[c], preferred_element_type=F32)

        for dr in range(2):
            mask = (ri <= ci) if dr == 0 else (ri >= ci)
            st0, w_rows = [], []
            for h in range(H):
                n = dr * H + h
                col_i = dr * 2 * H + h
                col_f = col_i + H
                b_row = bct[col_f:col_f + 1, :]
                i_row = gt[col_i:col_i + 1, :]
                g_tot = b_row[:, L - 1:L] if dr == 0 else b_row[:, 0:1]
                r_seen = jnp.where(mask, r_all[:, n * L:(n + 1) * L], neg_inf)
                cmax = jnp.max(r_seen, axis=0, keepdims=True)
                st0.append((kq[:, h * L:(h + 1) * L] * jnp.exp(r_seen - cmax)).astype(BF16))
                lw = g_tot - b_row + i_row
                m_loc = jnp.max(lw, axis=-1, keepdims=True)
                w_rows.append(jnp.exp(lw - m_loc))
                row_ref[dr, c, h:h + 1, :] = b_row + cmax
                row_ref[dr, c, H + h:H + h + 1, :] = jnp.broadcast_to(m_loc, (1, L))
                row_ref[dr, c, 2 * H + h:2 * H + h + 1, :] = jnp.broadcast_to(g_tot, (1, L))
                row_ref[dr, c, 3 * H + h:3 * H + h + 1, :] = b_row
            halves = []
            for p in range(H // 2):
                top = jnp.concatenate([st0[2 * p], zero_st], axis=-1)
                bot = jnp.concatenate([zero_st, st0[2 * p + 1]], axis=-1)
                halves.append(jnp.dot(vxb[:, 2 * p * L:(2 * p + 2) * L], jnp.concatenate([top, bot], axis=0),
                                      preferred_element_type=F32))
            tot0_ref[dr, c] = jnp.concatenate(halves, axis=-1)
            w_all = jnp.concatenate(w_rows, axis=-1)
            cloc_ref[dr, c] = jnp.dot((vx * w_all).astype(BF16), kbd_ref[c], preferred_element_type=F32)
        return 0

    lax.fori_loop(0, nc, prepare, 0)
    cst_ref[...] = jnp.zeros_like(cst_ref)

    def recur(step, m_states):
        new = []
        for dr in range(2):
            c = step if dr == 0 else nc - 1 - step
            c_st = cst_ref[dr]
            qc = jnp.dot(c_st.astype(BF16), qbd_ref[c], preferred_element_type=F32)
            tot0 = tot0_ref[dr, c]
            a_rows, b_rows = [], []
            for h in range(H):
                m_st = m_states[dr * H + h]
                hl = slice(h * L, (h + 1) * L)
                m_intra = row_ref[dr, c, h:h + 1, :]
                m_loc = row_ref[dr, c, H + h:H + h + 1, :]
                g_tot = row_ref[dr, c, 2 * H + h:2 * H + h + 1, :]
                b_row = row_ref[dr, c, 3 * H + h:3 * H + h + 1, :]
                m_out = jnp.maximum(m_intra, b_row + m_st)
                tot = tot0[:, hl] * jnp.exp(m_intra - m_out) + jnp.exp(b_row + m_st - m_out) * qc[:, hl]
                den = tot[D:D + 1, :]
                hout = tot[0:D, :] / jnp.maximum(jnp.abs(den), jnp.exp(-m_out))
                if dr == 0:
                    hf_ref[c, h * D:(h + 1) * D, :] = hout
                else:
                    hb_ref[c, h * D:(h + 1) * D, :] = hout
                m_new = jnp.maximum(g_tot + m_st, m_loc)
                a_rows.append(jnp.exp(g_tot + m_st - m_new))
                b_rows.append(jnp.exp(m_loc - m_new))
                new.append(m_new)
            cst_ref[dr] = per_head_rows(a_rows) * c_st + per_head_rows(b_rows) * cloc_ref[dr, c]
        return tuple(new)

    lax.fori_loop(0, nc, recur, tuple(jnp.zeros((1, L), F32) for _ in range(2 * H)))

    gm = gm_ref[...]
    for c in range(nc):
        rows = slice(c * L, (c + 1) * L)
        hsum = (hf_ref[c] + hb_ref[c]).T
        for j in range(MLSTM_WIDTH // LANES):
            sl = slice(j * LANES, (j + 1) * LANES)
            t = hsum[:, sl]
            y = t * lax.rsqrt(_dot_exact_rhs(t * t, gm) + EPS) * hg_ref[:, sl]
            y_ref[rows, sl] = y * jax.nn.sigmoid(o_ref[rows, sl])


def _mlstm(mq, mk, mv, mo, mg, fb_t, hg_row, gm, batch, seq):
    nc = seq // MLSTM_CHUNK
    L = MLSTM_CHUNK
    H = MLSTM_HEADS
    blk = lambda b: (b, 0)
    fixed = lambda b: (0, 0)
    wide = pl.BlockSpec((seq, MLSTM_WIDTH), blk)
    return pl.pallas_call(
        _mlstm_kernel,
        grid=(batch,),
        in_specs=[wide, wide, wide, wide,
                  pl.BlockSpec((seq, LANES), blk),
                  pl.BlockSpec((N_GATES, L), fixed),
                  pl.BlockSpec((1, MLSTM_WIDTH), fixed),
                  pl.BlockSpec((LANES, LANES), fixed)],
        out_specs=wide,
        out_shape=jax.ShapeDtypeStruct((batch * seq, MLSTM_WIDTH), F32),
        scratch_shapes=[
            pltpu.VMEM((nc, MLSTM_WIDTH, H * L), BF16),
            pltpu.VMEM((nc, H * L, MLSTM_WIDTH), BF16),
            pltpu.VMEM((nc, MLSTM_DX, H * L), F32),
            pltpu.VMEM((nc, MLSTM_DX, H * L), BF16),
            pltpu.VMEM((nc, N_GATES, L), F32),
            pltpu.VMEM((2, nc, 4 * H, L), F32),
            pltpu.VMEM((2, nc, MLSTM_DX, H * L), F32),
            pltpu.VMEM((2, nc, MLSTM_DX, MLSTM_WIDTH), F32),
            pltpu.VMEM((2, MLSTM_DX, MLSTM_WIDTH), F32),
            pltpu.VMEM((nc, MLSTM_WIDTH, L), F32),
            pltpu.VMEM((nc, MLSTM_WIDTH, L), F32),
        ],
        compiler_params=_params("parallel"),
        name="mlstm",
    )(mq, mk, mv, mo, mg, fb_t, hg_row, gm)


def _outproj_kernel(ya_ref, yl_ref, ym_ref, x_ref, ga_ref, gl_ref, wo_ref, g2_ref, wr_ref,
                    x1_ref, h2_ref, aff_ref):
    ya = _rms(ya_ref[...], ga_ref[...]).astype(BF16)
    yl = _rms(yl_ref[...], gl_ref[...]).astype(BF16)
    ym = ym_ref[...].astype(BF16)
    a0, a1 = ATT_WIDTH, ATT_WIDTH + LRU_WIDTH
    acc = jnp.dot(ya, wo_ref[0:a0, :], preferred_element_type=F32)
    acc = acc + jnp.dot(yl, wo_ref[a0:a1, :], preferred_element_type=F32)
    acc = acc + jnp.dot(ym, wo_ref[a1:, :], preferred_element_type=F32)
    x1 = x_ref[...] + acc
    x1_ref[...] = x1
    h2 = _rms(x1, g2_ref[...])
    h2_ref[...] = h2.astype(BF16)
    logits = _dot_nt_f32(wr_ref[...], h2)
    e = jnp.exp(logits - jnp.max(logits, axis=0, keepdims=True))
    aff_ref[0] = e / jnp.sum(e, axis=0, keepdims=True)


def _outproj(ya, yl, ym, x2d, ga, gl, wo, g2, wr_t, batch, seq):
    t = x2d.shape[0]
    tm = ROW_TILE
    n_seq = seq // tm
    row = lambda i: (i, 0)
    fixed = lambda i: (0, 0)
    return pl.pallas_call(
        _outproj_kernel,
        grid=(t // tm,),
        in_specs=[
            pl.BlockSpec((tm, ATT_WIDTH), row),
            pl.BlockSpec((tm, LRU_WIDTH), row),
            pl.BlockSpec((tm, MLSTM_WIDTH), row),
            pl.BlockSpec((tm, D_MODEL), row),
            pl.BlockSpec((1, ATT_WIDTH), fixed),
            pl.BlockSpec((1, LRU_WIDTH), fixed),
            pl.BlockSpec((D_MODEL, D_MODEL), fixed),
            pl.BlockSpec((1, D_MODEL), fixed),
            pl.BlockSpec((N_EXPERTS, D_MODEL), fixed),
        ],
        out_specs=[
            pl.BlockSpec((tm, D_MODEL), row),
            pl.BlockSpec((tm, D_MODEL), row),
            pl.BlockSpec((1, N_EXPERTS, tm), lambda i: (i // n_seq, 0, i % n_seq)),
        ],
        out_shape=[
            jax.ShapeDtypeStruct((t, D_MODEL), F32),
            jax.ShapeDtypeStruct((t, D_MODEL), BF16),
            jax.ShapeDtypeStruct((batch, N_EXPERTS, seq), F32),
        ],
        compiler_params=_params("parallel"),
        name="outproj",
    )(ya, yl, ym, x2d, ga, gl, wo, g2, wr_t)


def _route_kernel(aff_ref, pos_ref, post_ref, *, cap):
    a = aff_ref[...]
    rows, seq = a.shape
    as_float = lambda t: lax.bitcast_convert_type(t, F32)
    lo0 = jnp.zeros((rows, 1), I32)
    hi0 = jnp.full((rows, 1), 0x7F800000, I32)

    def count_ge(t):
        return jnp.sum(jnp.where(a >= as_float(t), 1.0, 0.0), axis=-1, keepdims=True)

    def bisect(_, c):
        lo, hi = c
        mid = lo + ((hi - lo) >> 1)
        ok = count_ge(mid) >= cap
        return jnp.where(ok, mid, lo), jnp.where(ok, hi, mid)

    thr, _ = lax.fori_loop(0, 31, bisect, (lo0, hi0))

    ri = lax.broadcasted_iota(I32, (LANES, LANES), 0)
    ci = lax.broadcasted_iota(I32, (LANES, LANES), 1)
    before = jnp.where(ri < ci, 1.0, 0.0).astype(BF16)

    def prefix_count(m):
        out = []
        off = jnp.zeros((rows, 1), F32)
        for j in range(seq // LANES):
            mb = m[:, j * LANES:(j + 1) * LANES]
            out.append(jnp.dot(mb.astype(BF16), before, preferred_element_type=F32) + off)
            off = off + jnp.sum(mb, axis=-1, keepdims=True)
        return jnp.concatenate(out, axis=-1)

    gt = jnp.where(a >= as_float(thr + 1), 1.0, 0.0)
    eq = jnp.where(a >= as_float(thr), 1.0, 0.0) - gt
    need = cap - jnp.sum(gt, axis=-1, keepdims=True)
    sel = gt + eq * jnp.where(prefix_count(eq) < need, 1.0, 0.0)
    pos = jnp.where(sel > 0.0, prefix_count(sel), -1.0)
    pos_ref[...] = pos.astype(I32)
    post_ref[...] = pos.T.astype(BF16)


def _route(aff2d, cap):
    rows, seq = aff2d.shape
    return pl.pallas_call(
        functools.partial(_route_kernel, cap=cap),
        out_shape=[jax.ShapeDtypeStruct((rows, seq), I32),
                   jax.ShapeDtypeStruct((seq, rows), BF16)],
        compiler_params=pltpu.CompilerParams(vmem_limit_bytes=VMEM_LIMIT),
        name="route",
    )(aff2d)


def _gather_kernel(h_ref, pos_ref, aff_ref, xg_ref, gate_ref, *, cap):
    pos = pos_ref[0]
    seq = pos.shape[-1]
    slot = lax.broadcasted_iota(I32, (cap, seq), 0)
    hit = slot == pos
    onehot = jnp.where(hit, 1.0, 0.0).astype(BF16)
    xg_ref[0] = jnp.dot(onehot, h_ref[...], preferred_element_type=F32).astype(BF16)
    gate_ref[0] = jnp.sum(jnp.where(hit, aff_ref[0], 0.0), axis=-1, keepdims=True)


def _gather(h2, pos3, aff3, batch, seq, cap):
    idx = lambda b, e: (b * N_EXPERTS + e, 0, 0)
    return pl.pallas_call(
        functools.partial(_gather_kernel, cap=cap),
        grid=(batch, N_EXPERTS),
        in_specs=[
            pl.BlockSpec((seq, D_MODEL), lambda b, e: (b, 0)),
            pl.BlockSpec((1, 1, seq), idx),
            pl.BlockSpec((1, 1, seq), idx),
        ],
        out_specs=[
            pl.BlockSpec((1, cap, D_MODEL), lambda b, e: (e, b, 0)),
            pl.BlockSpec((1, cap, 1), lambda b, e: (e, b, 0)),
        ],
        out_shape=[
            jax.ShapeDtypeStruct((N_EXPERTS, batch * cap, D_MODEL), BF16),
            jax.ShapeDtypeStruct((N_EXPERTS, batch * cap, 1), F32),
        ],
        compiler_params=_params("parallel", "parallel"),
        name="moe_gather",
    )(h2, pos3, aff3)


def _ffn_kernel(xg_ref, wg_ref, wu_ref, wd_ref, gate_ref, y_ref, acc_ref, wgb_ref, wub_ref, wdb_ref):
    f = pl.program_id(1)
    nf = pl.num_programs(1)
    rows = xg_ref.shape[1]
    wgb_ref[...] = wg_ref[0, 0].astype(BF16)
    wub_ref[...] = wu_ref[0, 0].astype(BF16)
    wdb_ref[...] = wd_ref[0, 0].astype(BF16)

    @pl.when(f == 0)
    def _():
        acc_ref[...] = jnp.zeros_like(acc_ref)

    for m in range(rows // FFN_M_TILE):
        sl = slice(m * FFN_M_TILE, (m + 1) * FFN_M_TILE)
        xs = xg_ref[0, sl, :]
        a = jnp.dot(xs, wgb_ref[...], preferred_element_type=F32)
        u = jnp.dot(xs, wub_ref[...], preferred_element_type=F32)
        hm = (a * jax.nn.sigmoid(a) * u).astype(BF16)
        acc_ref[sl, :] += jnp.dot(hm, wdb_ref[...], preferred_element_type=F32)

    @pl.when(f == nf - 1)
    def _():
        y_ref[0] = (acc_ref[...] * gate_ref[0]).astype(BF16)


def _ffn(xg, gate, w_gate, w_up, w_down, layer):
    n_e, rows, _ = xg.shape
    fc = FFN_F_TILE
    return pl.pallas_call(
        _ffn_kernel,
        grid=(n_e, D_EXPERT // fc),
        in_specs=[
            pl.BlockSpec((1, rows, D_MODEL), lambda e, f: (e, 0, 0)),
            pl.BlockSpec((1, 1, D_MODEL, fc), lambda e, f: (layer, e, 0, f)),
            pl.BlockSpec((1, 1, D_MODEL, fc), lambda e, f: (layer, e, 0, f)),
            pl.BlockSpec((1, 1, fc, D_MODEL), lambda e, f: (layer, e, f, 0)),
            pl.BlockSpec((1, rows, 1), lambda e, f: (e, 0, 0)),
        ],
        out_specs=pl.BlockSpec((1, rows, D_MODEL), lambda e, f: (e, 0, 0)),
        out_shape=jax.ShapeDtypeStruct((n_e, rows, D_MODEL), BF16),
        scratch_shapes=[
            pltpu.VMEM((rows, D_MODEL), F32),
            pltpu.VMEM((D_MODEL, fc), BF16),
            pltpu.VMEM((D_MODEL, fc), BF16),
            pltpu.VMEM((fc, D_MODEL), BF16),
        ],
        compiler_params=_params("parallel", "arbitrary"),
        name="moe_ffn",
    )(xg, w_gate, w_up, w_down, gate)


def _combine_kernel(x_ref, y_ref, post_ref, o_ref, *, cap):
    b = pl.program_id(0)
    rows = o_ref.shape[0]
    n_be = post_ref.shape[1]
    kk = N_EXPERTS * cap
    pick_row = lax.broadcasted_iota(I32, (n_be, kk), 0)
    pick_col = lax.broadcasted_iota(I32, (n_be, kk), 1) // cap
    pick = jnp.where(pick_row == b * N_EXPERTS + pick_col, 1.0, 0.0).astype(BF16)
    slot_of_row = jnp.dot(post_ref[...], pick, preferred_element_type=F32)
    lane = (lax.broadcasted_iota(I32, (rows, kk), 1) % cap).astype(F32)
    onehot = jnp.where(slot_of_row == lane, 1.0, 0.0).astype(BF16)
    y = y_ref[...].reshape(kk, y_ref.shape[-1])
    o_ref[...] = x_ref[...] + jnp.dot(onehot, y, preferred_element_type=F32)


def _combine(x1, yg, post, batch, seq, cap):
    rt = COMBINE_ROWS
    nh = seq // rt
    return pl.pallas_call(
        functools.partial(_combine_kernel, cap=cap),
        grid=(batch, nh),
        in_specs=[
            pl.BlockSpec((rt, D_MODEL), lambda b, r: (b * nh + r, 0)),
            pl.BlockSpec((N_EXPERTS, cap, D_MODEL), lambda b, r: (0, b, 0)),
            pl.BlockSpec((rt, batch * N_EXPERTS), lambda b, r: (r, 0)),
        ],
        out_specs=pl.BlockSpec((rt, D_MODEL), lambda b, r: (b * nh + r, 0)),
        out_shape=jax.ShapeDtypeStruct((batch * seq, D_MODEL), F32),
        compiler_params=_params("parallel", "parallel"),
        name="moe_combine",
    )(x1, yg, post)


def _rope_tables(seq):
    t = jnp.arange(seq)
    row = (t // GRID_W).astype(F32)
    col = (t % GRID_W).astype(F32)
    n_freq = HEAD_DIM // 4
    inv = 1.0 / (ROPE_THETA ** (jnp.arange(n_freq, dtype=F32) / n_freq))
    ang = jnp.concatenate([row[:, None] * inv, col[:, None] * inv], axis=-1)
    cos, sin = jnp.cos(ang), jnp.sin(ang)
    reps = LANES // HEAD_DIM
    cos_t = jnp.tile(jnp.concatenate([cos, cos], axis=-1), (1, reps))
    sin_t = jnp.tile(jnp.concatenate([-sin, sin], axis=-1), (1, reps))
    return cos_t, sin_t


def _group_mean_matrix():
    g = np.arange(LANES) // HEAD_DIM
    return jnp.asarray((g[:, None] == g[None, :]).astype(np.float32) / HEAD_DIM, dtype=BF16)


def _block_diag(w):
    n, d, _ = w.shape
    eye = jnp.eye(n, dtype=w.dtype)
    return jnp.einsum('nde,nm->ndme', w, eye).reshape(n * d, n * d)


def kernel(x, norm1_g, w_in, q_norm_g, k_norm_g, conv_w, conv_b, lru_wa, lru_ba, lru_wx, lru_bx,
           lru_lambda, mlstm_f_bias, mlstm_norm_g, att_out_g, lru_out_g, w_out, norm2_g, w_router,
           w_expert_gate, w_expert_up, w_expert_down):
    batch, seq, _ = x.shape
    depth = w_in.shape[0]
    cap = CAPACITY_FACTOR * seq // N_EXPERTS
    cos_t, sin_t = _rope_tables(seq)
    gm = _group_mean_matrix()
    reps = LANES // HEAD_DIM
    x2d = x.reshape(batch * seq, D_MODEL)

    for l in range(depth):
        w_pad = jnp.pad(w_in[l], ((0, 0), (0, D_IN_PAD - D_IN))).astype(BF16)
        q, k, v, lx, lg, mq, mk, mv, mo, mg = _inproj(
            x2d, norm1_g[l][None, :], w_pad, cos_t, sin_t,
            jnp.tile(q_norm_g[l], reps)[None, :], jnp.tile(k_norm_g[l], reps)[None, :], gm, seq)

        y_att = _attention(q, k, v, batch, seq)

        w_bd = jnp.concatenate([_block_diag(lru_wa[l, 0]), _block_diag(lru_wx[l, 0]),
                                _block_diag(lru_wa[l, 1]), _block_diag(lru_wx[l, 1])], axis=1).astype(BF16)
        b_all = jnp.concatenate([lru_ba[l, 0], lru_bx[l, 0], lru_ba[l, 1], lru_bx[l, 1]])[None, :]
        y_lru = _lru(lx, lg, conv_w[l], conv_b[l][None, :], w_bd, b_all, lru_lambda[l], batch, seq)

        zeros_h = jnp.zeros((MLSTM_HEADS,), F32)
        fb_col = jnp.concatenate([zeros_h, mlstm_f_bias[l, 0], zeros_h, mlstm_f_bias[l, 1]])
        fb_t = jnp.broadcast_to(fb_col[:, None], (N_GATES, MLSTM_CHUNK))
        y_mls = _mlstm(mq, mk, mv, mo, mg, fb_t, mlstm_norm_g[l].reshape(1, MLSTM_WIDTH), gm, batch, seq)

        x1, h2, aff = _outproj(y_att, y_lru, y_mls, x2d, att_out_g[l][None, :], lru_out_g[l][None, :],
                               w_out[l].astype(BF16), norm2_g[l][None, :], w_router[l].T, batch, seq)

        aff2d = aff.reshape(batch * N_EXPERTS, seq)
        pos, post = _route(aff2d, cap)
        xg, gate = _gather(h2, pos.reshape(batch * N_EXPERTS, 1, seq),
                           aff2d.reshape(batch * N_EXPERTS, 1, seq), batch, seq, cap)
        yg = _ffn(xg, gate, w_expert_gate, w_expert_up, w_expert_down, l)
        x2d = _combine(x1, yg, post, batch, seq, cap)

    return x2d.reshape(batch, seq, D_MODEL)
```

```python
import functools

import numpy as np
import jax
import jax.numpy as jnp
from jax import lax
from jax.experimental import pallas as pl
from jax.experimental.pallas import tpu as pltpu

F32 = jnp.float32
BF16 = jnp.bfloat16
I32 = jnp.int32

D_MODEL = 1024
GRID_W = 64
HEAD_DIM = 64
ATT_HEADS = 8
ATT_KV_HEADS = 2
ATT_GROUP = ATT_HEADS // ATT_KV_HEADS
ATT_WIDTH = ATT_HEADS * HEAD_DIM
KV_WIDTH = ATT_KV_HEADS * HEAD_DIM
ROPE_THETA = 10000.0
LRU_WIDTH = 256
LRU_BLOCKS = 4
LRU_BLOCK_DIM = LRU_WIDTH // LRU_BLOCKS
CONV_WIDTH = 4
LRU_C = 8.0
MLSTM_HEADS = 4
MLSTM_DIM = 64
MLSTM_WIDTH = MLSTM_HEADS * MLSTM_DIM
MLSTM_CHUNK = 128
MLSTM_DX = MLSTM_DIM + 8
N_GATES = 4 * MLSTM_HEADS
N_EXPERTS = 16
CAPACITY_FACTOR = 2
D_EXPERT = 2 * D_MODEL
EPS = 1e-6

LANES = 128
D_IN = ATT_WIDTH + 2 * KV_WIDTH + 2 * LRU_WIDTH + 4 * MLSTM_WIDTH + N_GATES
D_IN_PAD = ATT_WIDTH + 2 * KV_WIDTH + 2 * LRU_WIDTH + 4 * MLSTM_WIDTH + LANES
VMEM_LIMIT = 56 * 1024 * 1024

ROW_TILE = 512
ATT_K_TILE = 1024
ATT_Q_TILE = 256
FFN_F_TILE = 512
FFN_M_TILE = 512
COMBINE_ROWS = 512

NT_DIMS = (((1,), (1,)), ((), ()))


def _params(*sem):
    return pltpu.CompilerParams(dimension_semantics=sem, vmem_limit_bytes=VMEM_LIMIT)


def _split3(x):
    hi = x.astype(BF16)
    r1 = x - hi.astype(F32)
    mid = r1.astype(BF16)
    lo = (r1 - mid.astype(F32)).astype(BF16)
    return hi, mid, lo


def _dot_exact_rhs(x, m):
    hi, mid, lo = _split3(x)
    d = lambda a: jnp.dot(a, m, preferred_element_type=F32)
    return d(hi) + d(mid) + d(lo)


def _dot_nt_f32(a, b):
    ah, am, _ = _split3(a)
    bh, bm, _ = _split3(b)
    d = lambda p, q: lax.dot_general(p, q, NT_DIMS, preferred_element_type=F32)
    return d(ah, bh) + (d(ah, bm) + d(am, bh))


def _rms(x, g):
    return x * lax.rsqrt(jnp.mean(x * x, axis=-1, keepdims=True) + EPS) * g


def _softplus(x):
    return jnp.maximum(x, 0.0) + jnp.log1p(jnp.exp(-jnp.abs(x)))


def _inproj_kernel(x_ref, g_ref, w_ref, cos_ref, sin_ref, qg_ref, kg_ref, gm_ref,
                   q_ref, k_ref, v_ref, lx_ref, lg_ref, mq_ref, mk_ref, mv_ref, mo_ref, mg_ref):
    h = _rms(x_ref[...], g_ref[...]).astype(BF16)
    p = jnp.dot(h, w_ref[...], preferred_element_type=F32)
    cos = cos_ref[...]
    sin = sin_ref[...]
    gm = gm_ref[...]
    lane = lax.broadcasted_iota(I32, cos.shape, 1)
    first_half = (lane % HEAD_DIM) < (HEAD_DIM // 2)

    def norm_rope(t, gain):
        y = t * lax.rsqrt(_dot_exact_rhs(t * t, gm) + EPS) * gain
        swapped = jnp.where(first_half, pltpu.roll(y, LANES - HEAD_DIM // 2, 1),
                            pltpu.roll(y, HEAD_DIM // 2, 1))
        return y * cos + swapped * sin

    scale = HEAD_DIM ** -0.5
    for j in range(ATT_WIDTH // LANES):
        sl = slice(j * LANES, (j + 1) * LANES)
        q_ref[:, sl] = (norm_rope(p[:, sl], qg_ref[...]) * scale).astype(BF16)
    o = ATT_WIDTH
    k_ref[...] = norm_rope(p[:, o:o + KV_WIDTH], kg_ref[...]).astype(BF16)
    o += KV_WIDTH
    v_ref[...] = p[:, o:o + KV_WIDTH].astype(BF16)
    o += KV_WIDTH
    for ref in (lx_ref, lg_ref, mq_ref, mk_ref, mv_ref, mo_ref):
        ref[...] = p[:, o:o + LRU_WIDTH]
        o += LRU_WIDTH
    mg_ref[...] = p[:, o:o + LANES]


def _inproj(x2d, g, w_pad, cos_t, sin_t, qg, kg, gm, seq):
    t = x2d.shape[0]
    tm = ROW_TILE
    n_seq = seq // tm
    row = lambda i: (i, 0)
    fixed = lambda i: (0, 0)
    tab = lambda i: (i % n_seq, 0)
    widths = [(ATT_WIDTH, BF16), (KV_WIDTH, BF16), (KV_WIDTH, BF16)] + \
             [(LRU_WIDTH, F32)] * 6 + [(LANES, F32)]
    return pl.pallas_call(
        _inproj_kernel,
        grid=(t // tm,),
        in_specs=[
            pl.BlockSpec((tm, D_MODEL), row),
            pl.BlockSpec((1, D_MODEL), fixed),
            pl.BlockSpec((D_MODEL, D_IN_PAD), fixed),
            pl.BlockSpec((tm, LANES), tab),
            pl.BlockSpec((tm, LANES), tab),
            pl.BlockSpec((1, LANES), fixed),
            pl.BlockSpec((1, LANES), fixed),
            pl.BlockSpec((LANES, LANES), fixed),
        ],
        out_specs=[pl.BlockSpec((tm, w), row) for w, _ in widths],
        out_shape=[jax.ShapeDtypeStruct((t, w), dt) for w, dt in widths],
        compiler_params=_params("parallel"),
        name="inproj",
    )(x2d, g, w_pad, cos_t, sin_t, qg, kg, gm)


def _attn_kernel(q_ref, k_ref, v_ref, o_ref):
    tq = q_ref.shape[0]
    seq = k_ref.shape[0]
    kb = ATT_K_TILE
    for h in range(ATT_KV_HEADS):
        hl = slice(h * HEAD_DIM, (h + 1) * HEAD_DIM)
        heads = [h * ATT_GROUP + g for g in range(ATT_GROUP)]
        qs = jnp.concatenate([q_ref[:, a * HEAD_DIM:(a + 1) * HEAD_DIM] for a in heads], axis=0)
        m = l = acc = None
        for j in range(seq // kb):
            ks = slice(j * kb, (j + 1) * kb)
            s = lax.dot_general(qs, k_ref[ks, hl], NT_DIMS, preferred_element_type=F32)
            m_blk = jnp.max(s, axis=-1, keepdims=True)
            if j == 0:
                m = m_blk
                p = jnp.exp(s - m)
                l = jnp.sum(p, axis=-1, keepdims=True)
                acc = jnp.dot(p.astype(BF16), v_ref[ks, hl], preferred_element_type=F32)
            else:
                m_new = jnp.maximum(m, m_blk)
                alpha = jnp.exp(m - m_new)
                p = jnp.exp(s - m_new)
                l = alpha * l + jnp.sum(p, axis=-1, keepdims=True)
                acc = alpha * acc + jnp.dot(p.astype(BF16), v_ref[ks, hl], preferred_element_type=F32)
                m = m_new
        o = acc / l
        for g, a in enumerate(heads):
            o_ref[:, a * HEAD_DIM:(a + 1) * HEAD_DIM] = o[g * tq:(g + 1) * tq]


def _attention(q, k, v, batch, seq):
    tq = ATT_Q_TILE
    nq = seq // tq
    return pl.pallas_call(
        _attn_kernel,
        grid=(batch, nq),
        in_specs=[
            pl.BlockSpec((tq, ATT_WIDTH), lambda b, i: (b * nq + i, 0)),
            pl.BlockSpec((seq, KV_WIDTH), lambda b, i: (b, 0)),
            pl.BlockSpec((seq, KV_WIDTH), lambda b, i: (b, 0)),
        ],
        out_specs=pl.BlockSpec((tq, ATT_WIDTH), lambda b, i: (b * nq + i, 0)),
        out_shape=jax.ShapeDtypeStruct((batch * seq, ATT_WIDTH), F32),
        compiler_params=_params("parallel", "parallel"),
        name="attention",
    )(q, k, v)


def _lru_kernel(x_ref, gate_ref, cw_ref, cb_ref, w_ref, b_ref, lam_ref, o_ref):
    x = x_ref[...]
    seq = x.shape[0]
    row = lax.broadcasted_iota(I32, x.shape, 0)

    def earlier(a, d, fill):
        return jnp.where(row >= d, pltpu.roll(a, d, 0), fill)

    def later(a, d, fill):
        return jnp.where(row < seq - d, pltpu.roll(a, seq - d, 0), fill)

    xc = cb_ref[...] + earlier(x, 2, 0.0) * cw_ref[0:1, :]
    xc = xc + earlier(x, 1, 0.0) * cw_ref[1:2, :]
    xc = xc + x * cw_ref[2:3, :]
    xc = xc + later(x, 1, 0.0) * cw_ref[3:4, :]

    z = jnp.dot(xc.astype(BF16), w_ref[...], preferred_element_type=F32) + b_ref[...]
    h = None
    for dr, shift in enumerate((earlier, later)):
        o = dr * 2 * LRU_WIDTH
        r = jax.nn.sigmoid(z[:, o:o + LRU_WIDTH])
        i = jax.nn.sigmoid(z[:, o + LRU_WIDTH:o + 2 * LRU_WIDTH])
        log_a = -LRU_C * r * _softplus(-lam_ref[dr:dr + 1, :])
        a = jnp.exp(log_a)
        u = jnp.sqrt(1.0 - jnp.exp(2.0 * log_a)) * (i * xc)
        d = 1
        while d < seq:
            u = a * shift(u, d, 0.0) + u
            if 2 * d < seq:
                a = a * shift(a, d, 1.0)
            d *= 2
        h = u if h is None else h + u
    o_ref[...] = h * jax.nn.gelu(gate_ref[...])


def _lru(lx, lg, cw, cb, w_bd, b_all, lam, batch, seq):
    blk = lambda b: (b, 0)
    fixed = lambda b: (0, 0)
    return pl.pallas_call(
        _lru_kernel,
        grid=(batch,),
        in_specs=[
            pl.BlockSpec((seq, LRU_WIDTH), blk),
            pl.BlockSpec((seq, LRU_WIDTH), blk),
            pl.BlockSpec((CONV_WIDTH, LRU_WIDTH), fixed),
            pl.BlockSpec((1, LRU_WIDTH), fixed),
            pl.BlockSpec((LRU_WIDTH, 4 * LRU_WIDTH), fixed),
            pl.BlockSpec((1, 4 * LRU_WIDTH), fixed),
            pl.BlockSpec((2, LRU_WIDTH), fixed),
        ],
        out_specs=pl.BlockSpec((seq, LRU_WIDTH), blk),
        out_shape=jax.ShapeDtypeStruct((batch * seq, LRU_WIDTH), F32),
        compiler_params=_params("parallel"),
        name="rglru",
    )(lx, lg, cw, cb, w_bd, b_all, lam)


def _mlstm_kernel(q_ref, k_ref, v_ref, o_ref, g_ref, fbt_ref, hg_ref, gm_ref, y_ref,
                  qbd_ref, kbd_ref, vx_ref, vxb_ref, bct_ref, row_ref, tot0_ref, cloc_ref, cst_ref, hf_ref, hb_ref):
    seq = q_ref.shape[0]
    L = MLSTM_CHUNK
    D = MLSTM_DIM
    DX = MLSTM_DX
    H = MLSTM_HEADS
    NG = N_GATES
    nc = seq // L
    ri = lax.broadcasted_iota(I32, (L, L), 0)
    ci = lax.broadcasted_iota(I32, (L, L), 1)
    tri_pre = jnp.where(ri <= ci, 1.0, 0.0).astype(BF16)
    tri_suf = jnp.where(ri >= ci, 1.0, 0.0).astype(BF16)
    gr = ri[:NG]
    is_f = (gr // H) % 2 == 1
    is_fwd = gr < 2 * H
    neg_inf = jnp.float32(-jnp.inf)
    ones_row = jnp.where(lax.broadcasted_iota(I32, (DX - D, L), 0) == 0, 1.0, 0.0)
    pad_rows = jnp.zeros((L - 3 * NG, L), F32)
    zero_qblk = jnp.zeros((D, L), BF16)
    zero_st = jnp.zeros((L, L), BF16)
    lane_w = lax.broadcasted_iota(I32, (1, H * D), 1)
    sel_r = lax.broadcasted_iota(I32, (L, 2 * H * L), 0)
    sel_n = lax.broadcasted_iota(I32, (L, 2 * H * L), 1) // L
    sel_col = (sel_n // H) * 2 * H + H + sel_n % H
    sel_all = jnp.where(sel_r % NG == sel_col, jnp.where(sel_r < 3 * NG, 1.0, 0.0), 0.0).astype(BF16)
    head_of_lane = lax.broadcasted_iota(I32, (L, H * D), 1) // D

    def per_head_rows(vals):
        out = jnp.zeros((1, H * D), F32)
        for h, v in enumerate(vals):
            v2 = jnp.concatenate([v] * (H * D // L), axis=-1)
            out = jnp.where(lane_w // D == h, v2, out)
        return out

    def prepare(c, _):
        rows = pl.ds(pl.multiple_of(c * L, L), L)
        gt = g_ref[rows, :].T[:NG]
        lf = jnp.where(is_f, -_softplus(-(gt + fbt_ref[...])), 0.0)
        bct = jnp.where(is_fwd, _dot_exact_rhs(lf, tri_pre), _dot_exact_rhs(lf, tri_suf))
        rt = pltpu.roll(gt, H, 0) - bct
        bct_ref[c] = bct
        hi, mid, lo = _split3(rt)
        stacked = jnp.concatenate([hi.astype(F32), mid.astype(F32), lo.astype(F32), pad_rows], axis=0)
        r_all = jnp.dot(stacked.T.astype(BF16), sel_all, preferred_element_type=F32)

        qt = q_ref[rows, :].T.astype(BF16)
        for h in range(H):
            blocks = [zero_qblk] * H
            blocks[h] = qt[h * D:(h + 1) * D]
            qbd_ref[c, h * D:(h + 1) * D, :] = jnp.concatenate(blocks, axis=-1)
        ks = (k_ref[rows, :] * (D ** -0.5)).astype(BF16)
        for h in range(H):
            kbd_ref[c, h * L:(h + 1) * L, :] = jnp.where(head_of_lane == h, ks, jnp.zeros_like(ks))
        vt = v_ref[rows, :].T
        vx = jnp.concatenate([jnp.concatenate([vt[h * D:(h + 1) * D], ones_row], axis=0) for h in range(H)], axis=-1)
        vxb = vx.astype(BF16)
        vx_ref[c] = vx
        vxb_ref[c] = vxb
        kq = jnp.dot(ks, qbd_ref[c], preferred_element_type=F32)

        for dr in range(2):
            mask = (ri <= ci) if dr == 0 else (ri >= ci)
            st0, w_rows = [], []
            for h in range(H):
                n = dr * H + h
                col_i = dr * 2 * H + h
                col_f = col_i + H
                b_row = bct[col_f:col_f + 1, :]
                i_row = gt[col_i:col_i + 1, :]
                g_tot = b_row[:, L - 1:L] if dr == 0 else b_row[:, 0:1]
                r_seen = jnp.where(mask, r_all[:, n * L:(n + 1) * L], neg_inf)
                cmax = jnp.max(r_seen, axis=0, keepdims=True)
                st0.append((kq[:, h * L:(h + 1) * L] * jnp.exp(r_seen - cmax)).astype(BF16))
                lw = g_tot - b_row + i_row
                m_loc = jnp.max(lw, axis=-1, keepdims=True)
                w_rows.append(jnp.exp(lw - m_loc))
                row_ref[dr, c, h:h + 1, :] = b_row + cmax
                row_ref[dr, c, H + h:H + h + 1, :] = jnp.broadcast_to(m_loc, (1, L))
                row_ref[dr, c, 2 * H + h:2 * H + h + 1, :] = jnp.broadcast_to(g_tot, (1, L))
                row_ref[dr, c, 3 * H + h:3 * H + h + 1, :] = b_row
            halves = []
            for p in range(H // 2):
                top = jnp.concatenate([st0[2 * p], zero_st], axis=-1)
                bot = jnp.concatenate([zero_st, st0[2 * p + 1]], axis=-1)
                halves.append(jnp.dot(vxb[:, 2 * p * L:(2 * p + 2) * L], jnp.concatenate([top, bot], axis=0),
                                      preferred_element_type=F32))
            tot0_ref[dr, c] = jnp.concatenate(halves, axis=-1)
            w_all = jnp.concatenate(w_rows, axis=-1)
            cloc_ref[dr, c] = jnp.dot((vx * w_all).astype(BF16), kbd_ref[c], preferred_element_type=F32)
        return 0

    lax.fori_loop(0, nc, prepare, 0, unroll=2)
    cst_ref[...] = jnp.zeros_like(cst_ref)

    def recur(step, m_states):
        new = []
        for dr in range(2):
            c = step if dr == 0 else nc - 1 - step
            c_st = cst_ref[dr]
            qc = jnp.dot(c_st.astype(BF16), qbd_ref[c], preferred_element_type=F32)
            tot0 = tot0_ref[dr, c]
            a_rows, b_rows = [], []
            for h in range(H):
                m_st = m_states[dr * H + h]
                hl = slice(h * L, (h + 1) * L)
                m_intra = row_ref[dr, c, h:h + 1, :]
                m_loc = row_ref[dr, c, H + h:H + h + 1, :]
                g_tot = row_ref[dr, c, 2 * H + h:2 * H + h + 1, :]
                b_row = row_ref[dr, c, 3 * H + h:3 * H + h + 1, :]
                m_out = jnp.maximum(m_intra, b_row + m_st)
                tot = tot0[:, hl] * jnp.exp(m_intra - m_out) + jnp.exp(b_row + m_st - m_out) * qc[:, hl]
                den = tot[D:D + 1, :]
                hout = tot[0:D, :] / jnp.maximum(jnp.abs(den), jnp.exp(-m_out))
                if dr == 0:
                    hf_ref[c, h * D:(h + 1) * D, :] = hout
                else:
                    hb_ref[c, h * D:(h + 1) * D, :] = hout
                m_new = jnp.maximum(g_tot + m_st, m_loc)
                a_rows.append(jnp.exp(g_tot + m_st - m_new))
                b_rows.append(jnp.exp(m_loc - m_new))
                new.append(m_new)
            cst_ref[dr] = per_head_rows(a_rows) * c_st + per_head_rows(b_rows) * cloc_ref[dr, c]
        return tuple(new)

    lax.fori_loop(0, nc, recur, tuple(jnp.zeros((1, L), F32) for _ in range(2 * H)))

    gm = gm_ref[...]
    for c in range(nc):
        rows = slice(c * L, (c + 1) * L)
        hsum = (hf_ref[c] + hb_ref[c]).T
        for j in range(MLSTM_WIDTH // LANES):
            sl = slice(j * LANES, (j + 1) * LANES)
            t = hsum[:, sl]
            y = t * lax.rsqrt(_dot_exact_rhs(t * t, gm) + EPS) * hg_ref[:, sl]
            y_ref[rows, sl] = y * jax.nn.sigmoid(o_ref[rows, sl])


def _mlstm(mq, mk, mv, mo, mg, fb_t, hg_row, gm, batch, seq):
    nc = seq // MLSTM_CHUNK
    L = MLSTM_CHUNK
    H = MLSTM_HEADS
    blk = lambda b: (b, 0)
    fixed = lambda b: (0, 0)
    wide = pl.BlockSpec((seq, MLSTM_WIDTH), blk)
    return pl.pallas_call(
        _mlstm_kernel,
        grid=(batch,),
        in_specs=[wide, wide, wide, wide,
                  pl.BlockSpec((seq, LANES), blk),
                  pl.BlockSpec((N_GATES, L), fixed),
                  pl.BlockSpec((1, MLSTM_WIDTH), fixed),
                  pl.BlockSpec((LANES, LANES), fixed)],
        out_specs=wide,
        out_shape=jax.ShapeDtypeStruct((batch * seq, MLSTM_WIDTH), F32),
        scratch_shapes=[
            pltpu.VMEM((nc, MLSTM_WIDTH, H * L), BF16),
            pltpu.VMEM((nc, H * L, MLSTM_WIDTH), BF16),
            pltpu.VMEM((nc, MLSTM_DX, H * L), F32),
            pltpu.VMEM((nc, MLSTM_DX, H * L), BF16),
            pltpu.VMEM((nc, N_GATES, L), F32),
            pltpu.VMEM((2, nc, 4 * H, L), F32),
            pltpu.VMEM((2, nc, MLSTM_DX, H * L), F32),
            pltpu.VMEM((2, nc, MLSTM_DX, MLSTM_WIDTH), F32),
            pltpu.VMEM((2, MLSTM_DX, MLSTM_WIDTH), F32),
            pltpu.VMEM((nc, MLSTM_WIDTH, L), F32),
            pltpu.VMEM((nc, MLSTM_WIDTH, L), F32),
        ],
        compiler_params=_params("parallel"),
        name="mlstm",
    )(mq, mk, mv, mo, mg, fb_t, hg_row, gm)


def _outproj_kernel(ya_ref, yl_ref, ym_ref, x_ref, ga_ref, gl_ref, wo_ref, g2_ref, wr_ref,
                    x1_ref, h2_ref, aff_ref):
    ya = _rms(ya_ref[...], ga_ref[...]).astype(BF16)
    yl = _rms(yl_ref[...], gl_ref[...]).astype(BF16)
    ym = ym_ref[...].astype(BF16)
    a0, a1 = ATT_WIDTH, ATT_WIDTH + LRU_WIDTH
    acc = jnp.dot(ya, wo_ref[0:a0, :], preferred_element_type=F32)
    acc = acc + jnp.dot(yl, wo_ref[a0:a1, :], preferred_element_type=F32)
    acc = acc + jnp.dot(ym, wo_ref[a1:, :], preferred_element_type=F32)
    x1 = x_ref[...] + acc
    x1_ref[...] = x1
    h2 = _rms(x1, g2_ref[...])
    h2_ref[...] = h2.astype(BF16)
    logits = _dot_nt_f32(wr_ref[...], h2)
    e = jnp.exp(logits - jnp.max(logits, axis=0, keepdims=True))
    aff_ref[0] = e / jnp.sum(e, axis=0, keepdims=True)


def _outproj(ya, yl, ym, x2d, ga, gl, wo, g2, wr_t, batch, seq):
    t = x2d.shape[0]
    tm = ROW_TILE
    n_seq = seq // tm
    row = lambda i: (i, 0)
    fixed = lambda i: (0, 0)
    return pl.pallas_call(
        _outproj_kernel,
        grid=(t // tm,),
        in_specs=[
            pl.BlockSpec((tm, ATT_WIDTH), row),
            pl.BlockSpec((tm, LRU_WIDTH), row),
            pl.BlockSpec((tm, MLSTM_WIDTH), row),
            pl.BlockSpec((tm, D_MODEL), row),
            pl.BlockSpec((1, ATT_WIDTH), fixed),
            pl.BlockSpec((1, LRU_WIDTH), fixed),
            pl.BlockSpec((D_MODEL, D_MODEL), fixed),
            pl.BlockSpec((1, D_MODEL), fixed),
            pl.BlockSpec((N_EXPERTS, D_MODEL), fixed),
        ],
        out_specs=[
            pl.BlockSpec((tm, D_MODEL), row),
            pl.BlockSpec((tm, D_MODEL), row),
            pl.BlockSpec((1, N_EXPERTS, tm), lambda i: (i // n_seq, 0, i % n_seq)),
        ],
        out_shape=[
            jax.ShapeDtypeStruct((t, D_MODEL), F32),
            jax.ShapeDtypeStruct((t, D_MODEL), BF16),
            jax.ShapeDtypeStruct((batch, N_EXPERTS, seq), F32),
        ],
        compiler_params=_params("parallel"),
        name="outproj",
    )(ya, yl, ym, x2d, ga, gl, wo, g2, wr_t)


def _route_kernel(aff_ref, pos_ref, post_ref, *, cap):
    a = aff_ref[...]
    rows, seq = a.shape
    as_float = lambda t: lax.bitcast_convert_type(t, F32)
    lo0 = jnp.zeros((rows, 1), I32)
    hi0 = jnp.full((rows, 1), 0x7F800000, I32)

    def count_ge(t):
        return jnp.sum(jnp.where(a >= as_float(t), 1.0, 0.0), axis=-1, keepdims=True)

    def bisect(_, c):
        lo, hi = c
        mid = lo + ((hi - lo) >> 1)
        ok = count_ge(mid) >= cap
        return jnp.where(ok, mid, lo), jnp.where(ok, hi, mid)

    thr, _ = lax.fori_loop(0, 31, bisect, (lo0, hi0))

    ri = lax.broadcasted_iota(I32, (LANES, LANES), 0)
    ci = lax.broadcasted_iota(I32, (LANES, LANES), 1)
    before = jnp.where(ri < ci, 1.0, 0.0).astype(BF16)

    def prefix_count(m):
        out = []
        off = jnp.zeros((rows, 1), F32)
        for j in range(seq // LANES):
            mb = m[:, j * LANES:(j + 1) * LANES]
            out.append(jnp.dot(mb.astype(BF16), before, preferred_element_type=F32) + off)
            off = off + jnp.sum(mb, axis=-1, keepdims=True)
        return jnp.concatenate(out, axis=-1)

    gt = jnp.where(a >= as_float(thr + 1), 1.0, 0.0)
    eq = jnp.where(a >= as_float(thr), 1.0, 0.0) - gt
    need = cap - jnp.sum(gt, axis=-1, keepdims=True)
    sel = gt + eq * jnp.where(prefix_count(eq) < need, 1.0, 0.0)
    pos = jnp.where(sel > 0.0, prefix_count(sel), -1.0)
    pos_ref[...] = pos.astype(I32)
    post_ref[...] = pos.T.astype(BF16)


def _route(aff2d, cap):
    rows, seq = aff2d.shape
    return pl.pallas_call(
        functools.partial(_route_kernel, cap=cap),
        out_shape=[jax.ShapeDtypeStruct((rows, seq), I32),
                   jax.ShapeDtypeStruct((seq, rows), BF16)],
        compiler_params=pltpu.CompilerParams(vmem_limit_bytes=VMEM_LIMIT),
        name="route",
    )(aff2d)


def _gather_kernel(h_ref, pos_ref, aff_ref, xg_ref, gate_ref, *, cap):
    pos = pos_ref[0]
    seq = pos.shape[-1]
    slot = lax.broadcasted_iota(I32, (cap, seq), 0)
    hit = slot == pos
    onehot = jnp.where(hit, 1.0, 0.0).astype(BF16)
    xg_ref[0] = jnp.dot(onehot, h_ref[...], preferred_element_type=F32).astype(BF16)
    gate_ref[0] = jnp.sum(jnp.where(hit, aff_ref[0], 0.0), axis=-1, keepdims=True)


def _gather(h2, pos3, aff3, batch, seq, cap):
    idx = lambda b, e: (b * N_EXPERTS + e, 0, 0)
    return pl.pallas_call(
        functools.partial(_gather_kernel, cap=cap),
        grid=(batch, N_EXPERTS),
        in_specs=[
            pl.BlockSpec((seq, D_MODEL), lambda b, e: (b, 0)),
            pl.BlockSpec((1, 1, seq), idx),
            pl.BlockSpec((1, 1, seq), idx),
        ],
        out_specs=[
            pl.BlockSpec((1, cap, D_MODEL), lambda b, e: (e, b, 0)),
            pl.BlockSpec((1, cap, 1), lambda b, e: (e, b, 0)),
        ],
        out_shape=[
            jax.ShapeDtypeStruct((N_EXPERTS, batch * cap, D_MODEL), BF16),
            jax.ShapeDtypeStruct((N_EXPERTS, batch * cap, 1), F32),
        ],
        compiler_params=_params("parallel", "parallel"),
        name="moe_gather",
    )(h2, pos3, aff3)


def _ffn_kernel(xg_ref, wg_ref, wu_ref, wd_ref, gate_ref, y_ref, acc_ref, wgb_ref, wub_ref, wdb_ref):
    f = pl.program_id(1)
    nf = pl.num_programs(1)
    rows = xg_ref.shape[1]
    wgb_ref[...] = wg_ref[0, 0].astype(BF16)
    wub_ref[...] = wu_ref[0, 0].astype(BF16)
    wdb_ref[...] = wd_ref[0, 0].astype(BF16)

    @pl.when(f == 0)
    def _():
        acc_ref[...] = jnp.zeros_like(acc_ref)

    for m in range(rows // FFN_M_TILE):
        sl = slice(m * FFN_M_TILE, (m + 1) * FFN_M_TILE)
        xs = xg_ref[0, sl, :]
        a = jnp.dot(xs, wgb_ref[...], preferred_element_type=F32)
        u = jnp.dot(xs, wub_ref[...], preferred_element_type=F32)
        hm = (a * jax.nn.sigmoid(a) * u).astype(BF16)
        acc_ref[sl, :] += jnp.dot(hm, wdb_ref[...], preferred_element_type=F32)

    @pl.when(f == nf - 1)
    def _():
        y_ref[0] = (acc_ref[...] * gate_ref[0]).astype(BF16)


def _ffn(xg, gate, w_gate, w_up, w_down, layer):
    n_e, rows, _ = xg.shape
    fc = FFN_F_TILE
    return pl.pallas_call(
        _ffn_kernel,
        grid=(n_e, D_EXPERT // fc),
        in_specs=[
            pl.BlockSpec((1, rows, D_MODEL), lambda e, f: (e, 0, 0)),
            pl.BlockSpec((1, 1, D_MODEL, fc), lambda e, f: (layer, e, 0, f)),
            pl.BlockSpec((1, 1, D_MODEL, fc), lambda e, f: (layer, e, 0, f)),
            pl.BlockSpec((1, 1, fc, D_MODEL), lambda e, f: (layer, e, f, 0)),
            pl.BlockSpec((1, rows, 1), lambda e, f: (e, 0, 0)),
        ],
        out_specs=pl.BlockSpec((1, rows, D_MODEL), lambda e, f: (e, 0, 0)),
        out_shape=jax.ShapeDtypeStruct((n_e, rows, D_MODEL), BF16),
        scratch_shapes=[
            pltpu.VMEM((rows, D_MODEL), F32),
            pltpu.VMEM((D_MODEL, fc), BF16),
            pltpu.VMEM((D_MODEL, fc), BF16),
            pltpu.VMEM((fc, D_MODEL), BF16),
        ],
        compiler_params=_params("parallel", "arbitrary"),
        name="moe_ffn",
    )(xg, w_gate, w_up, w_down, gate)


def _combine_kernel(x_ref, y_ref, post_ref, o_ref, *, cap):
    b = pl.program_id(0)
    rows = o_ref.shape[0]
    n_be = post_ref.shape[1]
    kk = N_EXPERTS * cap
    pick_row = lax.broadcasted_iota(I32, (n_be, kk), 0)
    pick_col = lax.broadcasted_iota(I32, (n_be, kk), 1) // cap
    pick = jnp.where(pick_row == b * N_EXPERTS + pick_col, 1.0, 0.0).astype(BF16)
    slot_of_row = jnp.dot(post_ref[...], pick, preferred_element_type=F32)
    lane = (lax.broadcasted_iota(I32, (rows, kk), 1) % cap).astype(F32)
    onehot = jnp.where(slot_of_row == lane, 1.0, 0.0).astype(BF16)
    y = y_ref[...].reshape(kk, y_ref.shape[-1])
    o_ref[...] = x_ref[...] + jnp.dot(onehot, y, preferred_element_type=F32)


def _combine(x1, yg, post, batch, seq, cap):
    rt = COMBINE_ROWS
    nh = seq // rt
    return pl.pallas_call(
        functools.partial(_combine_kernel, cap=cap),
        grid=(batch, nh),
        in_specs=[
            pl.BlockSpec((rt, D_MODEL), lambda b, r: (b * nh + r, 0)),
            pl.BlockSpec((N_EXPERTS, cap, D_MODEL), lambda b, r: (0, b, 0)),
            pl.BlockSpec((rt, batch * N_EXPERTS), lambda b, r: (r, 0)),
        ],
        out_specs=pl.BlockSpec((rt, D_MODEL), lambda b, r: (b * nh + r, 0)),
        out_shape=jax.ShapeDtypeStruct((batch * seq, D_MODEL), F32),
        compiler_params=_params("parallel", "parallel"),
        name="moe_combine",
    )(x1, yg, post)


def _rope_tables(seq):
    t = jnp.arange(seq)
    row = (t // GRID_W).astype(F32)
    col = (t % GRID_W).astype(F32)
    n_freq = HEAD_DIM // 4
    inv = 1.0 / (ROPE_THETA ** (jnp.arange(n_freq, dtype=F32) / n_freq))
    ang = jnp.concatenate([row[:, None] * inv, col[:, None] * inv], axis=-1)
    cos, sin = jnp.cos(ang), jnp.sin(ang)
    reps = LANES // HEAD_DIM
    cos_t = jnp.tile(jnp.concatenate([cos, cos], axis=-1), (1, reps))
    sin_t = jnp.tile(jnp.concatenate([-sin, sin], axis=-1), (1, reps))
    return cos_t, sin_t


def _group_mean_matrix():
    g = np.arange(LANES) // HEAD_DIM
    return jnp.asarray((g[:, None] == g[None, :]).astype(np.float32) / HEAD_DIM, dtype=BF16)


def _block_diag(w):
    n, d, _ = w.shape
    eye = jnp.eye(n, dtype=w.dtype)
    return jnp.einsum('nde,nm->ndme', w, eye).reshape(n * d, n * d)


def kernel(x, norm1_g, w_in, q_norm_g, k_norm_g, conv_w, conv_b, lru_wa, lru_ba, lru_wx, lru_bx,
           lru_lambda, mlstm_f_bias, mlstm_norm_g, att_out_g, lru_out_g, w_out, norm2_g, w_router,
           w_expert_gate, w_expert_up, w_expert_down):
    batch, seq, _ = x.shape
    depth = w_in.shape[0]
    cap = CAPACITY_FACTOR * seq // N_EXPERTS
    cos_t, sin_t = _rope_tables(seq)
    gm = _group_mean_matrix()
    reps = LANES // HEAD_DIM
    x2d = x.reshape(batch * seq, D_MODEL)

    for l in range(depth):
        w_pad = jnp.pad(w_in[l], ((0, 0), (0, D_IN_PAD - D_IN))).astype(BF16)
        q, k, v, lx, lg, mq, mk, mv, mo, mg = _inproj(
            x2d, norm1_g[l][None, :], w_pad, cos_t, sin_t,
            jnp.tile(q_norm_g[l], reps)[None, :], jnp.tile(k_norm_g[l], reps)[None, :], gm, seq)

        y_att = _attention(q, k, v, batch, seq)

        w_bd = jnp.concatenate([_block_diag(lru_wa[l, 0]), _block_diag(lru_wx[l, 0]),
                                _block_diag(lru_wa[l, 1]), _block_diag(lru_wx[l, 1])], axis=1).astype(BF16)
        b_all = jnp.concatenate([lru_ba[l, 0], lru_bx[l, 0], lru_ba[l, 1], lru_bx[l, 1]])[None, :]
        y_lru = _lru(lx, lg, conv_w[l], conv_b[l][None, :], w_bd, b_all, lru_lambda[l], batch, seq)

        zeros_h = jnp.zeros((MLSTM_HEADS,), F32)
        fb_col = jnp.concatenate([zeros_h, mlstm_f_bias[l, 0], zeros_h, mlstm_f_bias[l, 1]])
        fb_t = jnp.broadcast_to(fb_col[:, None], (N_GATES, MLSTM_CHUNK))
        y_mls = _mlstm(mq, mk, mv, mo, mg, fb_t, mlstm_norm_g[l].reshape(1, MLSTM_WIDTH), gm, batch, seq)

        x1, h2, aff = _outproj(y_att, y_lru, y_mls, x2d, att_out_g[l][None, :], lru_out_g[l][None, :],
                               w_out[l].astype(BF16), norm2_g[l][None, :], w_router[l].T, batch, seq)

        aff2d = aff.reshape(batch * N_EXPERTS, seq)
        pos, post = _route(aff2d, cap)
        xg, gate = _gather(h2, pos.reshape(batch * N_EXPERTS, 1, seq),
                           aff2d.reshape(batch * N_EXPERTS, 1, seq), batch, seq, cap)
        yg = _ffn(xg, gate, w_expert_gate, w_expert_up, w_expert_down, l)
        x2d = _combine(x1, yg, post, batch, seq, cap)

    return x2d.reshape(batch, seq, D_MODEL)
```

```python
import functools

import numpy as np
import jax
import jax.numpy as jnp
from jax import lax
from jax.experimental import pallas as pl
from jax.experimental.pallas import tpu as pltpu

F32 = jnp.float32
BF16 = jnp.bfloat16
I32 = jnp.int32

D_MODEL = 1024
GRID_W = 64
HEAD_DIM = 64
ATT_HEADS = 8
ATT_KV_HEADS = 2
ATT_GROUP = ATT_HEADS // ATT_KV_HEADS
ATT_WIDTH = ATT_HEADS * HEAD_DIM
KV_WIDTH = ATT_KV_HEADS * HEAD_DIM
ROPE_THETA = 10000.0
LRU_WIDTH = 256
LRU_BLOCKS = 4
LRU_BLOCK_DIM = LRU_WIDTH // LRU_BLOCKS
CONV_WIDTH = 4
LRU_C = 8.0
MLSTM_HEADS = 4
MLSTM_DIM = 64
MLSTM_WIDTH = MLSTM_HEADS * MLSTM_DIM
MLSTM_CHUNK = 128
MLSTM_DX = MLSTM_DIM + 8
N_GATES = 4 * MLSTM_HEADS
N_EXPERTS = 16
CAPACITY_FACTOR = 2
D_EXPERT = 2 * D_MODEL
EPS = 1e-6

LANES = 128
D_IN = ATT_WIDTH + 2 * KV_WIDTH + 2 * LRU_WIDTH + 4 * MLSTM_WIDTH + N_GATES
D_IN_PAD = ATT_WIDTH + 2 * KV_WIDTH + 2 * LRU_WIDTH + 4 * MLSTM_WIDTH + LANES
VMEM_LIMIT = 56 * 1024 * 1024

ROW_TILE = 512
ATT_K_TILE = 1024
ATT_Q_TILE = 256
FFN_F_TILE = 512
FFN_M_TILE = 512
COMBINE_ROWS = 512

NT_DIMS = (((1,), (1,)), ((), ()))


def _params(*sem):
    return pltpu.CompilerParams(dimension_semantics=sem, vmem_limit_bytes=VMEM_LIMIT)


def _split3(x):
    hi = x.astype(BF16)
    r1 = x - hi.astype(F32)
    mid = r1.astype(BF16)
    lo = (r1 - mid.astype(F32)).astype(BF16)
    return hi, mid, lo


def _dot_exact_rhs(x, m):
    hi, mid, lo = _split3(x)
    d = lambda a: jnp.dot(a, m, preferred_element_type=F32)
    return d(hi) + d(mid) + d(lo)


def _dot_nt_f32(a, b):
    ah, am, _ = _split3(a)
    bh, bm, _ = _split3(b)
    d = lambda p, q: lax.dot_general(p, q, NT_DIMS, preferred_element_type=F32)
    return d(ah, bh) + (d(ah, bm) + d(am, bh))


def _rms(x, g):
    return x * lax.rsqrt(jnp.mean(x * x, axis=-1, keepdims=True) + EPS) * g


def _softplus(x):
    return jnp.maximum(x, 0.0) + jnp.log1p(jnp.exp(-jnp.abs(x)))


def _inproj_kernel(x_ref, g_ref, w_ref, cos_ref, sin_ref, qg_ref, kg_ref, gm_ref,
                   q_ref, k_ref, v_ref, lx_ref, lg_ref, mq_ref, mk_ref, mv_ref, mo_ref, mg_ref):
    h = _rms(x_ref[...], g_ref[...]).astype(BF16)
    p = jnp.dot(h, w_ref[...], preferred_element_type=F32)
    cos = cos_ref[...]
    sin = sin_ref[...]
    gm = gm_ref[...]
    lane = lax.broadcasted_iota(I32, cos.shape, 1)
    first_half = (lane % HEAD_DIM) < (HEAD_DIM // 2)

    def norm_rope(t, gain):
        y = t * lax.rsqrt(_dot_exact_rhs(t * t, gm) + EPS) * gain
        swapped = jnp.where(first_half, pltpu.roll(y, LANES - HEAD_DIM // 2, 1),
                            pltpu.roll(y, HEAD_DIM // 2, 1))
        return y * cos + swapped * sin

    scale = HEAD_DIM ** -0.5
    for j in range(ATT_WIDTH // LANES):
        sl = slice(j * LANES, (j + 1) * LANES)
        q_ref[:, sl] = (norm_rope(p[:, sl], qg_ref[...]) * scale).astype(BF16)
    o = ATT_WIDTH
    k_ref[...] = norm_rope(p[:, o:o + KV_WIDTH], kg_ref[...]).astype(BF16)
    o += KV_WIDTH
    v_ref[...] = p[:, o:o + KV_WIDTH].astype(BF16)
    o += KV_WIDTH
    for ref in (lx_ref, lg_ref, mq_ref, mk_ref, mv_ref, mo_ref):
        ref[...] = p[:, o:o + LRU_WIDTH]
        o += LRU_WIDTH
    mg_ref[...] = p[:, o:o + LANES]


def _inproj(x2d, g, w_pad, cos_t, sin_t, qg, kg, gm, seq):
    t = x2d.shape[0]
    tm = ROW_TILE
    n_seq = seq // tm
    row = lambda i: (i, 0)
    fixed = lambda i: (0, 0)
    tab = lambda i: (i % n_seq, 0)
    widths = [(ATT_WIDTH, BF16), (KV_WIDTH, BF16), (KV_WIDTH, BF16)] + \
             [(LRU_WIDTH, F32)] * 6 + [(LANES, F32)]
    return pl.pallas_call(
        _inproj_kernel,
        grid=(t // tm,),
        in_specs=[
            pl.BlockSpec((tm, D_MODEL), row),
            pl.BlockSpec((1, D_MODEL), fixed),
            pl.BlockSpec((D_MODEL, D_IN_PAD), fixed),
            pl.BlockSpec((tm, LANES), tab),
            pl.BlockSpec((tm, LANES), tab),
            pl.BlockSpec((1, LANES), fixed),
            pl.BlockSpec((1, LANES), fixed),
            pl.BlockSpec((LANES, LANES), fixed),
        ],
        out_specs=[pl.BlockSpec((tm, w), row) for w, _ in widths],
        out_shape=[jax.ShapeDtypeStruct((t, w), dt) for w, dt in widths],
        compiler_params=_params("parallel"),
        name="inproj",
    )(x2d, g, w_pad, cos_t, sin_t, qg, kg, gm)


def _attn_kernel(q_ref, k_ref, v_ref, o_ref):
    tq = q_ref.shape[0]
    seq = k_ref.shape[0]
    kb = ATT_K_TILE
    for h in range(ATT_KV_HEADS):
        hl = slice(h * HEAD_DIM, (h + 1) * HEAD_DIM)
        heads = [h * ATT_GROUP + g for g in range(ATT_GROUP)]
        qs = jnp.concatenate([q_ref[:, a * HEAD_DIM:(a + 1) * HEAD_DIM] for a in heads], axis=0)
        m = l = acc = None
        for j in range(seq // kb):
            ks = slice(j * kb, (j + 1) * kb)
            s = lax.dot_general(qs, k_ref[ks, hl], NT_DIMS, preferred_element_type=F32)
            m_blk = jnp.max(s, axis=-1, keepdims=True)
            if j == 0:
                m = m_blk
                p = jnp.exp((s - m).astype(BF16))
                l = jnp.sum(p.astype(F32), axis=-1, keepdims=True)
                acc = jnp.dot(p, v_ref[ks, hl], preferred_element_type=F32)
            else:
                m_new = jnp.maximum(m, m_blk)
                alpha = jnp.exp(m - m_new)
                p = jnp.exp((s - m_new).astype(BF16))
                l = alpha * l + jnp.sum(p.astype(F32), axis=-1, keepdims=True)
                acc = alpha * acc + jnp.dot(p, v_ref[ks, hl], preferred_element_type=F32)
                m = m_new
        o = acc / l
        for g, a in enumerate(heads):
            o_ref[:, a * HEAD_DIM:(a + 1) * HEAD_DIM] = o[g * tq:(g + 1) * tq]


def _attention(q, k, v, batch, seq):
    tq = ATT_Q_TILE
    nq = seq // tq
    return pl.pallas_call(
        _attn_kernel,
        grid=(batch, nq),
        in_specs=[
            pl.BlockSpec((tq, ATT_WIDTH), lambda b, i: (b * nq + i, 0)),
            pl.BlockSpec((seq, KV_WIDTH), lambda b, i: (b, 0)),
            pl.BlockSpec((seq, KV_WIDTH), lambda b, i: (b, 0)),
        ],
        out_specs=pl.BlockSpec((tq, ATT_WIDTH), lambda b, i: (b * nq + i, 0)),
        out_shape=jax.ShapeDtypeStruct((batch * seq, ATT_WIDTH), F32),
        compiler_params=_params("parallel", "parallel"),
        name="attention",
    )(q, k, v)


def _lru_kernel(x_ref, gate_ref, cw_ref, cb_ref, w_ref, b_ref, lam_ref, o_ref):
    x = x_ref[...]
    seq = x.shape[0]
    row = lax.broadcasted_iota(I32, x.shape, 0)

    def earlier(a, d, fill):
        return jnp.where(row >= d, pltpu.roll(a, d, 0), fill)

    def later(a, d, fill):
        return jnp.where(row < seq - d, pltpu.roll(a, seq - d, 0), fill)

    xc = cb_ref[...] + earlier(x, 2, 0.0) * cw_ref[0:1, :]
    xc = xc + earlier(x, 1, 0.0) * cw_ref[1:2, :]
    xc = xc + x * cw_ref[2:3, :]
    xc = xc + later(x, 1, 0.0) * cw_ref[3:4, :]

    z = jnp.dot(xc.astype(BF16), w_ref[...], preferred_element_type=F32) + b_ref[...]
    h = None
    for dr, shift in enumerate((earlier, later)):
        o = dr * 2 * LRU_WIDTH
        r = jax.nn.sigmoid(z[:, o:o + LRU_WIDTH])
        i = jax.nn.sigmoid(z[:, o + LRU_WIDTH:o + 2 * LRU_WIDTH])
        log_a = -LRU_C * r * _softplus(-lam_ref[dr:dr + 1, :])
        a = jnp.exp(log_a)
        u = jnp.sqrt(1.0 - jnp.exp(2.0 * log_a)) * (i * xc)
        d = 1
        while d < seq:
            u = a * shift(u, d, 0.0) + u
            if 2 * d < seq:
                a = a * shift(a, d, 1.0)
            d *= 2
        h = u if h is None else h + u
    o_ref[...] = h * jax.nn.gelu(gate_ref[...])


def _lru(lx, lg, cw, cb, w_bd, b_all, lam, batch, seq):
    blk = lambda b: (b, 0)
    fixed = lambda b: (0, 0)
    return pl.pallas_call(
        _lru_kernel,
        grid=(batch,),
        in_specs=[
            pl.BlockSpec((seq, LRU_WIDTH), blk),
            pl.BlockSpec((seq, LRU_WIDTH), blk),
            pl.BlockSpec((CONV_WIDTH, LRU_WIDTH), fixed),
            pl.BlockSpec((1, LRU_WIDTH), fixed),
            pl.BlockSpec((LRU_WIDTH, 4 * LRU_WIDTH), fixed),
            pl.BlockSpec((1, 4 * LRU_WIDTH), fixed),
            pl.BlockSpec((2, LRU_WIDTH), fixed),
        ],
        out_specs=pl.BlockSpec((seq, LRU_WIDTH), blk),
        out_shape=jax.ShapeDtypeStruct((batch * seq, LRU_WIDTH), F32),
        compiler_params=_params("parallel"),
        name="rglru",
    )(lx, lg, cw, cb, w_bd, b_all, lam)


def _mlstm_kernel(q_ref, k_ref, v_ref, o_ref, g_ref, fbt_ref, hg_ref, gm_ref, y_ref,
                  qbd_ref, kbd_ref, vx_ref, vxb_ref, bct_ref, row_ref, tot0_ref, cloc_ref, cst_ref, hf_ref, hb_ref):
    seq = q_ref.shape[0]
    L = MLSTM_CHUNK
    D = MLSTM_DIM
    DX = MLSTM_DX
    H = MLSTM_HEADS
    NG = N_GATES
    nc = seq // L
    ri = lax.broadcasted_iota(I32, (L, L), 0)
    ci = lax.broadcasted_iota(I32, (L, L), 1)
    tri_pre = jnp.where(ri <= ci, 1.0, 0.0).astype(BF16)
    tri_suf = jnp.where(ri >= ci, 1.0, 0.0).astype(BF16)
    gr = ri[:NG]
    is_f = (gr // H) % 2 == 1
    is_fwd = gr < 2 * H
    neg_inf = jnp.float32(-jnp.inf)
    ones_row = jnp.where(lax.broadcasted_iota(I32, (DX - D, L), 0) == 0, 1.0, 0.0)
    pad_rows = jnp.zeros((L - 3 * NG, L), F32)
    zero_qblk = jnp.zeros((D, L), BF16)
    zero_st = jnp.zeros((L, L), BF16)
    lane_w = lax.broadcasted_iota(I32, (1, H * D), 1)
    sel_r = lax.broadcasted_iota(I32, (L, 2 * H * L), 0)
    sel_n = lax.broadcasted_iota(I32, (L, 2 * H * L), 1) // L
    sel_col = (sel_n // H) * 2 * H + H + sel_n % H
    sel_all = jnp.where(sel_r % NG == sel_col, jnp.where(sel_r < 3 * NG, 1.0, 0.0), 0.0).astype(BF16)
    head_of_lane = lax.broadcasted_iota(I32, (L, H * D), 1) // D

    def per_head_rows(vals):
        out = jnp.zeros((1, H * D), F32)
        for h, v in enumerate(vals):
            v2 = jnp.concatenate([v] * (H * D // L), axis=-1)
            out = jnp.where(lane_w // D == h, v2, out)
        return out

    def prepare(c, _):
        rows = pl.ds(pl.multiple_of(c * L, L), L)
        gt = g_ref[rows, :].T[:NG]
        lf = jnp.where(is_f, -_softplus(-(gt + fbt_ref[...])), 0.0)
        bct = jnp.where(is_fwd, _dot_exact_rhs(lf, tri_pre), _dot_exact_rhs(lf, tri_suf))
        rt = pltpu.roll(gt, H, 0) - bct
        bct_ref[c] = bct
        hi, mid, lo = _split3(rt)
        stacked = jnp.concatenate([hi.astype(F32), mid.astype(F32), lo.astype(F32), pad_rows], axis=0)
        r_all = jnp.dot(stacked.T.astype(BF16), sel_all, preferred_element_type=F32)

        qt = q_ref[rows, :].T.astype(BF16)
        for h in range(H):
            blocks = [zero_qblk] * H
            blocks[h] = qt[h * D:(h + 1) * D]
            qbd_ref[c, h * D:(h + 1) * D, :] = jnp.concatenate(blocks, axis=-1)
        ks = (k_ref[rows, :] * (D ** -0.5)).astype(BF16)
        for h in range(H):
            kbd_ref[c, h * L:(h + 1) * L, :] = jnp.where(head_of_lane == h, ks, jnp.zeros_like(ks))
        vt = v_ref[rows, :].T
        vx = jnp.concatenate([jnp.concatenate([vt[h * D:(h + 1) * D], ones_row], axis=0) for h in range(H)], axis=-1)
        vxb = vx.astype(BF16)
        vx_ref[c] = vx
        vxb_ref[c] = vxb
        kq = jnp.dot(ks, qbd_ref[c], preferred_element_type=F32)

        for dr in range(2):
            mask = (ri <= ci) if dr == 0 else (ri >= ci)
            st0, w_rows = [], []
            for h in range(H):
                n = dr * H + h
                col_i = dr * 2 * H + h
                col_f = col_i + H
                b_row = bct[col_f:col_f + 1, :]
                i_row = gt[col_i:col_i + 1, :]
                g_tot = b_row[:, L - 1:L] if dr == 0 else b_row[:, 0:1]
                r_seen = jnp.where(mask, r_all[:, n * L:(n + 1) * L], neg_inf)
                cmax = jnp.max(r_seen, axis=0, keepdims=True)
                st0.append((kq[:, h * L:(h + 1) * L] * jnp.exp(r_seen - cmax)).astype(BF16))
                lw = g_tot - b_row + i_row
                m_loc = jnp.max(lw, axis=-1, keepdims=True)
                w_rows.append(jnp.exp(lw - m_loc))
                row_ref[dr, c, h:h + 1, :] = b_row + cmax
                row_ref[dr, c, H + h:H + h + 1, :] = jnp.broadcast_to(m_loc, (1, L))
                row_ref[dr, c, 2 * H + h:2 * H + h + 1, :] = jnp.broadcast_to(g_tot, (1, L))
                row_ref[dr, c, 3 * H + h:3 * H + h + 1, :] = b_row
            halves = []
            for p in range(H // 2):
                top = jnp.concatenate([st0[2 * p], zero_st], axis=-1)
                bot = jnp.concatenate([zero_st, st0[2 * p + 1]], axis=-1)
                halves.append(jnp.dot(vxb[:, 2 * p * L:(2 * p + 2) * L], jnp.concatenate([top, bot], axis=0),
                                      preferred_element_type=F32))
            tot0_ref[dr, c] = jnp.concatenate(halves, axis=-1)
            w_all = jnp.concatenate(w_rows, axis=-1)
            cloc_ref[dr, c] = jnp.dot((vx * w_all).astype(BF16), kbd_ref[c], preferred_element_type=F32)
        return 0

    lax.fori_loop(0, nc, prepare, 0, unroll=2)
    cst_ref[...] = jnp.zeros_like(cst_ref)

    def recur(step, m_states):
        new = []
        for dr in range(2):
            c = step if dr == 0 else nc - 1 - step
            c_st = cst_ref[dr]
            qc = jnp.dot(c_st.astype(BF16), qbd_ref[c], preferred_element_type=F32)
            tot0 = tot0_ref[dr, c]
            a_rows, b_rows = [], []
            for h in range(H):
                m_st = m_states[dr * H + h]
                hl = slice(h * L, (h + 1) * L)
                m_intra = row_ref[dr, c, h:h + 1, :]
                m_loc = row_ref[dr, c, H + h:H + h + 1, :]
                g_tot = row_ref[dr, c, 2 * H + h:2 * H + h + 1, :]
                b_row = row_ref[dr, c, 3 * H + h:3 * H + h + 1, :]
                m_out = jnp.maximum(m_intra, b_row + m_st)
                tot = tot0[:, hl] * jnp.exp(m_intra - m_out) + jnp.exp(b_row + m_st - m_out) * qc[:, hl]
                den = tot[D:D + 1, :]
                hout = tot[0:D, :] / jnp.maximum(jnp.abs(den), jnp.exp(-m_out))
                if dr == 0:
                    hf_ref[c, h * D:(h + 1) * D, :] = hout
                else:
                    hb_ref[c, h * D:(h + 1) * D, :] = hout
                m_new = jnp.maximum(g_tot + m_st, m_loc)
                a_rows.append(jnp.exp(g_tot + m_st - m_new))
                b_rows.append(jnp.exp(m_loc - m_new))
                new.append(m_new)
            cst_ref[dr] = per_head_rows(a_rows) * c_st + per_head_rows(b_rows) * cloc_ref[dr, c]
        return tuple(new)

    lax.fori_loop(0, nc, recur, tuple(jnp.zeros((1, L), F32) for _ in range(2 * H)))

    gm = gm_ref[...]
    for c in range(nc):
        rows = slice(c * L, (c + 1) * L)
        hsum = (hf_ref[c] + hb_ref[c]).T
        for j in range(MLSTM_WIDTH // LANES):
            sl = slice(j * LANES, (j + 1) * LANES)
            t = hsum[:, sl]
            y = t * lax.rsqrt(_dot_exact_rhs(t * t, gm) + EPS) * hg_ref[:, sl]
            y_ref[rows, sl] = y * jax.nn.sigmoid(o_ref[rows, sl])


def _mlstm(mq, mk, mv, mo, mg, fb_t, hg_row, gm, batch, seq):
    nc = seq // MLSTM_CHUNK
    L = MLSTM_CHUNK
    H = MLSTM_HEADS
    blk = lambda b: (b, 0)
    fixed = lambda b: (0, 0)
    wide = pl.BlockSpec((seq, MLSTM_WIDTH), blk)
    return pl.pallas_call(
        _mlstm_kernel,
        grid=(batch,),
        in_specs=[wide, wide, wide, wide,
                  pl.BlockSpec((seq, LANES), blk),
                  pl.BlockSpec((N_GATES, L), fixed),
                  pl.BlockSpec((1, MLSTM_WIDTH), fixed),
                  pl.BlockSpec((LANES, LANES), fixed)],
        out_specs=wide,
        out_shape=jax.ShapeDtypeStruct((batch * seq, MLSTM_WIDTH), F32),
        scratch_shapes=[
            pltpu.VMEM((nc, MLSTM_WIDTH, H * L), BF16),
            pltpu.VMEM((nc, H * L, MLSTM_WIDTH), BF16),
            pltpu.VMEM((nc, MLSTM_DX, H * L), F32),
            pltpu.VMEM((nc, MLSTM_DX, H * L), BF16),
            pltpu.VMEM((nc, N_GATES, L), F32),
            pltpu.VMEM((2, nc, 4 * H, L), F32),
            pltpu.VMEM((2, nc, MLSTM_DX, H * L), F32),
            pltpu.VMEM((2, nc, MLSTM_DX, MLSTM_WIDTH), F32),
            pltpu.VMEM((2, MLSTM_DX, MLSTM_WIDTH), F32),
            pltpu.VMEM((nc, MLSTM_WIDTH, L), F32),
            pltpu.VMEM((nc, MLSTM_WIDTH, L), F32),
        ],
        compiler_params=_params("parallel"),
        name="mlstm",
    )(mq, mk, mv, mo, mg, fb_t, hg_row, gm)


def _outproj_kernel(ya_ref, yl_ref, ym_ref, x_ref, ga_ref, gl_ref, wo_ref, g2_ref, wr_ref,
                    x1_ref, h2_ref, aff_ref):
    ya = _rms(ya_ref[...], ga_ref[...]).astype(BF16)
    yl = _rms(yl_ref[...], gl_ref[...]).astype(BF16)
    ym = ym_ref[...].astype(BF16)
    a0, a1 = ATT_WIDTH, ATT_WIDTH + LRU_WIDTH
    acc = jnp.dot(ya, wo_ref[0:a0, :], preferred_element_type=F32)
    acc = acc + jnp.dot(yl, wo_ref[a0:a1, :], preferred_element_type=F32)
    acc = acc + jnp.dot(ym, wo_ref[a1:, :], preferred_element_type=F32)
    x1 = x_ref[...] + acc
    x1_ref[...] = x1
    h2 = _rms(x1, g2_ref[...])
    h2_ref[...] = h2.astype(BF16)
    logits = _dot_nt_f32(wr_ref[...], h2)
    e = jnp.exp(logits - jnp.max(logits, axis=0, keepdims=True))
    aff_ref[0] = e / jnp.sum(e, axis=0, keepdims=True)


def _outproj(ya, yl, ym, x2d, ga, gl, wo, g2, wr_t, batch, seq):
    t = x2d.shape[0]
    tm = ROW_TILE
    n_seq = seq // tm
    row = lambda i: (i, 0)
    fixed = lambda i: (0, 0)
    return pl.pallas_call(
        _outproj_kernel,
        grid=(t // tm,),
        in_specs=[
            pl.BlockSpec((tm, ATT_WIDTH), row),
            pl.BlockSpec((tm, LRU_WIDTH), row),
            pl.BlockSpec((tm, MLSTM_WIDTH), row),
            pl.BlockSpec((tm, D_MODEL), row),
            pl.BlockSpec((1, ATT_WIDTH), fixed),
            pl.BlockSpec((1, LRU_WIDTH), fixed),
            pl.BlockSpec((D_MODEL, D_MODEL), fixed),
            pl.BlockSpec((1, D_MODEL), fixed),
            pl.BlockSpec((N_EXPERTS, D_MODEL), fixed),
        ],
        out_specs=[
            pl.BlockSpec((tm, D_MODEL), row),
            pl.BlockSpec((tm, D_MODEL), row),
            pl.BlockSpec((1, N_EXPERTS, tm), lambda i: (i // n_seq, 0, i % n_seq)),
        ],
        out_shape=[
            jax.ShapeDtypeStruct((t, D_MODEL), F32),
            jax.ShapeDtypeStruct((t, D_MODEL), BF16),
            jax.ShapeDtypeStruct((batch, N_EXPERTS, seq), F32),
        ],
        compiler_params=_params("parallel"),
        name="outproj",
    )(ya, yl, ym, x2d, ga, gl, wo, g2, wr_t)


def _route_kernel(aff_ref, pos_ref, post_ref, *, cap):
    a = aff_ref[...]
    rows, seq = a.shape
    as_float = lambda t: lax.bitcast_convert_type(t, F32)
    lo0 = jnp.zeros((rows, 1), I32)
    hi0 = jnp.full((rows, 1), 0x7F800000, I32)

    def count_ge(t):
        return jnp.sum(jnp.where(a >= as_float(t), 1.0, 0.0), axis=-1, keepdims=True)

    def bisect(_, c):
        lo, hi = c
        mid = lo + ((hi - lo) >> 1)
        ok = count_ge(mid) >= cap
        return jnp.where(ok, mid, lo), jnp.where(ok, hi, mid)

    thr, _ = lax.fori_loop(0, 31, bisect, (lo0, hi0))

    ri = lax.broadcasted_iota(I32, (LANES, LANES), 0)
    ci = lax.broadcasted_iota(I32, (LANES, LANES), 1)
    before = jnp.where(ri < ci, 1.0, 0.0).astype(BF16)

    def prefix_count(m):
        out = []
        off = jnp.zeros((rows, 1), F32)
        for j in range(seq // LANES):
            mb = m[:, j * LANES:(j + 1) * LANES]
            out.append(jnp.dot(mb.astype(BF16), before, preferred_element_type=F32) + off)
            off = off + jnp.sum(mb, axis=-1, keepdims=True)
        return jnp.concatenate(out, axis=-1)

    gt = jnp.where(a >= as_float(thr + 1), 1.0, 0.0)
    eq = jnp.where(a >= as_float(thr), 1.0, 0.0) - gt
    need = cap - jnp.sum(gt, axis=-1, keepdims=True)
    sel = gt + eq * jnp.where(prefix_count(eq) < need, 1.0, 0.0)
    pos = jnp.where(sel > 0.0, prefix_count(sel), -1.0)
    pos_ref[...] = pos.astype(I32)
    post_ref[...] = pos.T.astype(BF16)


def _route(aff2d, cap):
    rows, seq = aff2d.shape
    return pl.pallas_call(
        functools.partial(_route_kernel, cap=cap),
        out_shape=[jax.ShapeDtypeStruct((rows, seq), I32),
                   jax.ShapeDtypeStruct((seq, rows), BF16)],
        compiler_params=pltpu.CompilerParams(vmem_limit_bytes=VMEM_LIMIT),
        name="route",
    )(aff2d)


def _gather_kernel(h_ref, pos_ref, aff_ref, xg_ref, gate_ref, *, cap):
    pos = pos_ref[0]
    seq = pos.shape[-1]
    slot = lax.broadcasted_iota(I32, (cap, seq), 0)
    hit = slot == pos
    onehot = jnp.where(hit, 1.0, 0.0).astype(BF16)
    xg_ref[0] = jnp.dot(onehot, h_ref[...], preferred_element_type=F32).astype(BF16)
    gate_ref[0] = jnp.sum(jnp.where(hit, aff_ref[0], 0.0), axis=-1, keepdims=True)


def _gather(h2, pos3, aff3, batch, seq, cap):
    idx = lambda b, e: (b * N_EXPERTS + e, 0, 0)
    return pl.pallas_call(
        functools.partial(_gather_kernel, cap=cap),
        grid=(batch, N_EXPERTS),
        in_specs=[
            pl.BlockSpec((seq, D_MODEL), lambda b, e: (b, 0)),
            pl.BlockSpec((1, 1, seq), idx),
            pl.BlockSpec((1, 1, seq), idx),
        ],
        out_specs=[
            pl.BlockSpec((1, cap, D_MODEL), lambda b, e: (e, b, 0)),
            pl.BlockSpec((1, cap, 1), lambda b, e: (e, b, 0)),
        ],
        out_shape=[
            jax.ShapeDtypeStruct((N_EXPERTS, batch * cap, D_MODEL), BF16),
            jax.ShapeDtypeStruct((N_EXPERTS, batch * cap, 1), F32),
        ],
        compiler_params=_params("parallel", "parallel"),
        name="moe_gather",
    )(h2, pos3, aff3)


def _ffn_kernel(xg_ref, wg_ref, wu_ref, wd_ref, gate_ref, y_ref, acc_ref, wgb_ref, wub_ref, wdb_ref):
    f = pl.program_id(1)
    nf = pl.num_programs(1)
    rows = xg_ref.shape[1]
    wgb_ref[...] = wg_ref[0, 0].astype(BF16)
    wub_ref[...] = wu_ref[0, 0].astype(BF16)
    wdb_ref[...] = wd_ref[0, 0].astype(BF16)

    @pl.when(f == 0)
    def _():
        acc_ref[...] = jnp.zeros_like(acc_ref)

    for m in range(rows // FFN_M_TILE):
        sl = slice(m * FFN_M_TILE, (m + 1) * FFN_M_TILE)
        xs = xg_ref[0, sl, :]
        a = jnp.dot(xs, wgb_ref[...], preferred_element_type=F32)
        u = jnp.dot(xs, wub_ref[...], preferred_element_type=F32)
        hm = (a * jax.nn.sigmoid(a) * u).astype(BF16)
        acc_ref[sl, :] += jnp.dot(hm, wdb_ref[...], preferred_element_type=F32)

    @pl.when(f == nf - 1)
    def _():
        y_ref[0] = (acc_ref[...] * gate_ref[0]).astype(BF16)


def _ffn(xg, gate, w_gate, w_up, w_down, layer):
    n_e, rows, _ = xg.shape
    fc = FFN_F_TILE
    return pl.pallas_call(
        _ffn_kernel,
        grid=(n_e, D_EXPERT // fc),
        in_specs=[
            pl.BlockSpec((1, rows, D_MODEL), lambda e, f: (e, 0, 0)),
            pl.BlockSpec((1, 1, D_MODEL, fc), lambda e, f: (layer, e, 0, f)),
            pl.BlockSpec((1, 1, D_MODEL, fc), lambda e, f: (layer, e, 0, f)),
            pl.BlockSpec((1, 1, fc, D_MODEL), lambda e, f: (layer, e, f, 0)),
            pl.BlockSpec((1, rows, 1), lambda e, f: (e, 0, 0)),
        ],
        out_specs=pl.BlockSpec((1, rows, D_MODEL), lambda e, f: (e, 0, 0)),
        out_shape=jax.ShapeDtypeStruct((n_e, rows, D_MODEL), BF16),
        scratch_shapes=[
            pltpu.VMEM((rows, D_MODEL), F32),
            pltpu.VMEM((D_MODEL, fc), BF16),
            pltpu.VMEM((D_MODEL, fc), BF16),
            pltpu.VMEM((fc, D_MODEL), BF16),
        ],
        compiler_params=_params("parallel", "arbitrary"),
        name="moe_ffn",
    )(xg, w_gate, w_up, w_down, gate)


def _combine_kernel(x_ref, y_ref, post_ref, o_ref, *, cap):
    b = pl.program_id(0)
    rows = o_ref.shape[0]
    n_be = post_ref.shape[1]
    kk = N_EXPERTS * cap
    pick_row = lax.broadcasted_iota(I32, (n_be, kk), 0)
    pick_col = lax.broadcasted_iota(I32, (n_be, kk), 1) // cap
    pick = jnp.where(pick_row == b * N_EXPERTS + pick_col, 1.0, 0.0).astype(BF16)
    slot_of_row = jnp.dot(post_ref[...], pick, preferred_element_type=F32)
    lane = (lax.broadcasted_iota(I32, (rows, kk), 1) % cap).astype(F32)
    onehot = jnp.where(slot_of_row == lane, 1.0, 0.0).astype(BF16)
    y = y_ref[...].reshape(kk, y_ref.shape[-1])
    o_ref[...] = x_ref[...] + jnp.dot(onehot, y, preferred_element_type=F32)


def _combine(x1, yg, post, batch, seq, cap):
    rt = COMBINE_ROWS
    nh = seq // rt
    return pl.pallas_call(
        functools.partial(_combine_kernel, cap=cap),
        grid=(batch, nh),
        in_specs=[
            pl.BlockSpec((rt, D_MODEL), lambda b, r: (b * nh + r, 0)),
            pl.BlockSpec((N_EXPERTS, cap, D_MODEL), lambda b, r: (0, b, 0)),
            pl.BlockSpec((rt, batch * N_EXPERTS), lambda b, r: (r, 0)),
        ],
        out_specs=pl.BlockSpec((rt, D_MODEL), lambda b, r: (b * nh + r, 0)),
        out_shape=jax.ShapeDtypeStruct((batch * seq, D_MODEL), F32),
        compiler_params=_params("parallel", "parallel"),
        name="moe_combine",
    )(x1, yg, post)


def _rope_tables(seq):
    t = jnp.arange(seq)
    row = (t // GRID_W).astype(F32)
    col = (t % GRID_W).astype(F32)
    n_freq = HEAD_DIM // 4
    inv = 1.0 / (ROPE_THETA ** (jnp.arange(n_freq, dtype=F32) / n_freq))
    ang = jnp.concatenate([row[:, None] * inv, col[:, None] * inv], axis=-1)
    cos, sin = jnp.cos(ang), jnp.sin(ang)
    reps = LANES // HEAD_DIM
    cos_t = jnp.tile(jnp.concatenate([cos, cos], axis=-1), (1, reps))
    sin_t = jnp.tile(jnp.concatenate([-sin, sin], axis=-1), (1, reps))
    return cos_t, sin_t


def _group_mean_matrix():
    g = np.arange(LANES) // HEAD_DIM
    return jnp.asarray((g[:, None] == g[None, :]).astype(np.float32) / HEAD_DIM, dtype=BF16)


def _block_diag(w):
    n, d, _ = w.shape
    eye = jnp.eye(n, dtype=w.dtype)
    return jnp.einsum('nde,nm->ndme', w, eye).reshape(n * d, n * d)


def kernel(x, norm1_g, w_in, q_norm_g, k_norm_g, conv_w, conv_b, lru_wa, lru_ba, lru_wx, lru_bx,
           lru_lambda, mlstm_f_bias, mlstm_norm_g, att_out_g, lru_out_g, w_out, norm2_g, w_router,
           w_expert_gate, w_expert_up, w_expert_down):
    batch, seq, _ = x.shape
    depth = w_in.shape[0]
    cap = CAPACITY_FACTOR * seq // N_EXPERTS
    cos_t, sin_t = _rope_tables(seq)
    gm = _group_mean_matrix()
    reps = LANES // HEAD_DIM
    x2d = x.reshape(batch * seq, D_MODEL)

    for l in range(depth):
        w_pad = jnp.pad(w_in[l], ((0, 0), (0, D_IN_PAD - D_IN))).astype(BF16)
        q, k, v, lx, lg, mq, mk, mv, mo, mg = _inproj(
            x2d, norm1_g[l][None, :], w_pad, cos_t, sin_t,
            jnp.tile(q_norm_g[l], reps)[None, :], jnp.tile(k_norm_g[l], reps)[None, :], gm, seq)

        y_att = _attention(q, k, v, batch, seq)

        w_bd = jnp.concatenate([_block_diag(lru_wa[l, 0]), _block_diag(lru_wx[l, 0]),
                                _block_diag(lru_wa[l, 1]), _block_diag(lru_wx[l, 1])], axis=1).astype(BF16)
        b_all = jnp.concatenate([lru_ba[l, 0], lru_bx[l, 0], lru_ba[l, 1], lru_bx[l, 1]])[None, :]
        y_lru = _lru(lx, lg, conv_w[l], conv_b[l][None, :], w_bd, b_all, lru_lambda[l], batch, seq)

        zeros_h = jnp.zeros((MLSTM_HEADS,), F32)
        fb_col = jnp.concatenate([zeros_h, mlstm_f_bias[l, 0], zeros_h, mlstm_f_bias[l, 1]])
        fb_t = jnp.broadcast_to(fb_col[:, None], (N_GATES, MLSTM_CHUNK))
        y_mls = _mlstm(mq, mk, mv, mo, mg, fb_t, mlstm_norm_g[l].reshape(1, MLSTM_WIDTH), gm, batch, seq)

        x1, h2, aff = _outproj(y_att, y_lru, y_mls, x2d, att_out_g[l][None, :], lru_out_g[l][None, :],
                               w_out[l].astype(BF16), norm2_g[l][None, :], w_router[l].T, batch, seq)

        aff2d = aff.reshape(batch * N_EXPERTS, seq)
        pos, post = _route(aff2d, cap)
        xg, gate = _gather(h2, pos.reshape(batch * N_EXPERTS, 1, seq),
                           aff2d.reshape(batch * N_EXPERTS, 1, seq), batch, seq, cap)
        yg = _ffn(xg, gate, w_expert_gate, w_expert_up, w_expert_down, l)
        x2d = _combine(x1, yg, post, batch, seq, cap)

    return x2d.reshape(batch, seq, D_MODEL)
```
